```python
import math
import jax, jax.numpy as jnp
from jax import lax
import numpy as np

D_MODEL = 1024
BATCH = 16
SEQ = 4096
DEPTH = 1
DEC_BATCH = 128
DEC_SEQ = 1
PAST_LEN = 8192
PAGE_SIZE = 128

N_HEADS = 16
HEAD_DIM = 64
N_KV_HEADS = 4
GROUP = N_HEADS // N_KV_HEADS
CMP_BLK = 32
SEL_BLK = 64
N_SEL = 16
WINDOW = 512
N_BUCKETS = 32
MAX_DISTANCE = 1024
D_CONV = D_MODEL
CONV_K = 31
D_FF = 3 * D_MODEL
FFN_K = 3
Q_BLOCK = 32
EPS = 1e-6
FORCE_SCORE = 1e4
SPLITS = (2 * D_CONV, N_HEADS * HEAD_DIM, 6 * N_KV_HEADS * HEAD_DIM, 3 * N_HEADS, 2 * D_MODEL)
N_IN = sum(SPLITS)

kernel_name = "hybrid_conformer_nsa_decoder_step"


def rmsnorm(x, g):
    xf = x.astype(jnp.float32)
    r = lax.rsqrt(jnp.mean(xf * xf, axis=-1, keepdims=True) + EPS)
    return (xf * r).astype(x.dtype) * g


def layernorm(x, g, b):
    xf = x.astype(jnp.float32)
    mu = jnp.mean(xf, axis=-1, keepdims=True)
    var = jnp.mean(jnp.square(xf - mu), axis=-1, keepdims=True)
    return ((xf - mu) * lax.rsqrt(var + EPS)).astype(x.dtype) * g + b


def adaln(c, w, b):
    mod = (jax.nn.silu(c) @ w + b).reshape(c.shape[0], 6, D_MODEL)[:, :, None, :]
    return [mod[:, i] for i in range(6)]


def causal_dwconv(hist, u, w, b):
    full = jnp.concatenate([hist.astype(u.dtype), u], axis=1)
    out = lax.conv_general_dilated(full, w[:, None, :].astype(u.dtype), window_strides=(1,), padding='VALID',
                                   dimension_numbers=('NWC', 'WIO', 'NWC'), feature_group_count=u.shape[-1])
    return out + b


def t5_bucket(dist):
    d = jnp.maximum(dist, 0)
    max_exact = N_BUCKETS // 2
    df = jnp.maximum(d, 1).astype(jnp.float32)
    large = max_exact + (jnp.log(df / max_exact) / math.log(MAX_DISTANCE / max_exact)
                         * (N_BUCKETS - max_exact)).astype(jnp.int32)
    large = jnp.minimum(large, N_BUCKETS - 1)
    return jnp.where(d < max_exact, d, large)


def masked_softmax(logits, mask):
    lf = jnp.where(mask, logits.astype(jnp.float32), -1e30)
    m = jnp.max(lf, axis=-1, keepdims=True)
    e = jnp.where(mask, jnp.exp(lf - m), 0.0)
    return e / jnp.maximum(jnp.sum(e, axis=-1, keepdims=True), 1e-30)


def compress(x, pe, w1, w2):
    h = jax.nn.silu(jnp.einsum('...jd,de->...je', x + pe, w1))
    return jnp.einsum('...d,de->...e', jnp.mean(h, axis=-2), w2)


def blockify(rows):
    b, l = rows.shape[:2]
    return rows.reshape(b, l // CMP_BLK, CMP_BLK, N_KV_HEADS, HEAD_DIM).transpose(0, 1, 3, 2, 4)


def split_proj(h, w_in):
    b, t = h.shape[:2]
    offs = [int(v) for v in np.cumsum(SPLITS)[:-1]]
    u_in, q, kv, g_nsa, g_br = jnp.split(h @ w_in, offs, axis=-1)
    return (u_in, q.reshape(b, t, N_HEADS, HEAD_DIM), kv.reshape(b, t, 6, N_KV_HEADS, HEAD_DIM),
            g_nsa.reshape(b, t, N_HEADS, 3), g_br)


def nsa_core(q, pos_q, kc, vc, fetch_sel, n_sel_blocks, kw, vw, pos_w, gates, rel_bias):
    b, t = q.shape[:2]
    qg = q.reshape(b, t, N_KV_HEADS, GROUP, HEAD_DIM) * (HEAD_DIM ** -0.5)
    nc = kc.shape[1]
    pos_c = (jnp.arange(nc, dtype=jnp.int32) + 1) * CMP_BLK - 1
    dist_c = pos_q[:, None] - pos_c[None, :]
    bias_c = rel_bias[t5_bucket(dist_c)].reshape(t, nc, N_KV_HEADS, GROUP).transpose(0, 2, 3, 1)
    lc = jnp.einsum('btkgd,bckd->btkgc', qg, kc).astype(jnp.float32) + bias_c
    pc = masked_softmax(lc, (dist_c >= 0)[:, None, None, :])
    o_c = jnp.einsum('btkgc,bckd->btkgd', pc.astype(vc.dtype), vc)
    ratio = SEL_BLK // CMP_BLK
    imp = jnp.sum(pc, axis=3).reshape(b, t, N_KV_HEADS, n_sel_blocks, ratio).sum(-1)
    js = jnp.arange(n_sel_blocks, dtype=jnp.int32)[None, :]
    qblk = (pos_q // SEL_BLK)[:, None]
    valid = js * SEL_BLK <= pos_q[:, None]
    forced = (js == 0) | (js == qblk) | (js == qblk - 1)
    score = jnp.where(forced[:, None, :], FORCE_SCORE, jnp.where(valid[:, None, :], imp, -1.0))
    n_top = min(N_SEL, n_sel_blocks)
    _, idx = lax.top_k(score, n_top)
    ks, vs = fetch_sel(idx)
    pos_s = idx[..., None] * SEL_BLK + jnp.arange(SEL_BLK, dtype=jnp.int32)
    dist_s = pos_q[None, :, None, None, None] - pos_s
    table_kg = rel_bias.reshape(N_BUCKETS, N_KV_HEADS, GROUP).transpose(1, 0, 2)
    kk = jnp.arange(N_KV_HEADS)[None, None, :, None, None]
    bias_s = table_kg[kk, t5_bucket(dist_s)].transpose(0, 1, 2, 5, 3, 4)
    ls = jnp.einsum('btkgd,btknsd->btkgns', qg, ks).astype(jnp.float32) + bias_s
    nk = n_top * SEL_BLK
    ps = masked_softmax(ls.reshape(b, t, N_KV_HEADS, GROUP, nk), (dist_s >= 0).reshape(b, t, N_KV_HEADS, 1, nk))
    o_s = jnp.einsum('btkgx,btkxd->btkgd', ps.astype(vs.dtype), vs.reshape(b, t, N_KV_HEADS, nk, HEAD_DIM))
    nw = kw.shape[1]
    dist_w = pos_q[:, None] - pos_w[None, :]
    bias_w = rel_bias[t5_bucket(dist_w)].reshape(t, nw, N_KV_HEADS, GROUP).transpose(0, 2, 3, 1)
    lw = jnp.einsum('btkgd,bwkd->btkgw', qg, kw).astype(jnp.float32) + bias_w
    mask_w = (pos_w[None, :] >= 0) & (dist_w >= 0) & (dist_w <= WINDOW)
    pw = masked_softmax(lw, mask_w[:, None, None, :])
    o_w = jnp.einsum('btkgw,bwkd->btkgd', pw.astype(vw.dtype), vw)
    g = jax.nn.sigmoid(gates.astype(jnp.float32)).reshape(b, t, N_KV_HEADS, GROUP, 3)
    o = g[..., 0:1] * o_c + g[..., 1:2] * o_s + g[..., 2:3] * o_w
    return o.astype(q.dtype).reshape(b, t, N_HEADS * HEAD_DIM)


def nsa_prompt(q, kv, g_nsa, pe, w1, w2, rel_bias):
    b, s = q.shape[:2]
    kcmp = compress(blockify(kv[:, :, 0]), pe[0], w1[0], w2[0])
    vcmp = compress(blockify(kv[:, :, 1]), pe[1], w1[1], w2[1])
    ns = s // SEL_BLK
    ks_b = kv[:, :, 2].reshape(b, ns, SEL_BLK, N_KV_HEADS, HEAD_DIM).transpose(0, 3, 1, 2, 4)
    vs_b = kv[:, :, 3].reshape(b, ns, SEL_BLK, N_KV_HEADS, HEAD_DIM).transpose(0, 3, 1, 2, 4)
    bb = jnp.arange(b)[:, None, None, None]
    kk = jnp.arange(N_KV_HEADS)[None, None, :, None]

    def fetch(idx):
        return ks_b[bb, kk, idx], vs_b[bb, kk, idx]

    pad = ((0, 0), (WINDOW, 0), (0, 0), (0, 0))
    kw_pad = jnp.pad(kv[:, :, 4], pad)
    vw_pad = jnp.pad(kv[:, :, 5], pad)

    def block(i):
        s0 = i * Q_BLOCK
        qb = lax.dynamic_slice_in_dim(q, s0, Q_BLOCK, axis=1)
        gb = lax.dynamic_slice_in_dim(g_nsa, s0, Q_BLOCK, axis=1)
        kwb = lax.dynamic_slice_in_dim(kw_pad, s0, WINDOW + Q_BLOCK, axis=1)
        vwb = lax.dynamic_slice_in_dim(vw_pad, s0, WINDOW + Q_BLOCK, axis=1)
        pos_q = s0 + jnp.arange(Q_BLOCK, dtype=jnp.int32)
        pos_w = s0 - WINDOW + jnp.arange(WINDOW + Q_BLOCK, dtype=jnp.int32)
        return nsa_core(qb, pos_q, kcmp, vcmp, fetch, ns, kwb, vwb, pos_w, gb, rel_bias)

    out = lax.map(block, jnp.arange(s // Q_BLOCK, dtype=jnp.int32))
    return out.transpose(1, 0, 2, 3).reshape(b, s, N_HEADS * HEAD_DIM)


def nsa_sample(q, kv, g_nsa, cache_kv, l, page_table, win, pe, w1, w2, rel_bias):
    db, ds = q.shape[:2]
    n_pages = page_table.shape[1]
    past = PAST_LEN
    new_pad = -(-ds // SEL_BLK) * SEL_BLK
    kv_new = jnp.pad(kv, ((0, 0), (0, new_pad - ds), (0, 0), (0, 0), (0, 0)))

    def past_summary(slot):
        rows = cache_kv[page_table, l, slot]
        blk = rows.reshape(db, n_pages, N_KV_HEADS, PAGE_SIZE // CMP_BLK, CMP_BLK, HEAD_DIM)
        sm = compress(blk, pe[slot], w1[slot], w2[slot])
        return sm.transpose(0, 1, 3, 2, 4).reshape(db, n_pages * (PAGE_SIZE // CMP_BLK), N_KV_HEADS, HEAD_DIM)

    kcmp = jnp.concatenate([past_summary(0), compress(blockify(kv_new[:, :, 0]), pe[0], w1[0], w2[0])], axis=1)
    vcmp = jnp.concatenate([past_summary(1), compress(blockify(kv_new[:, :, 1]), pe[1], w1[1], w2[1])], axis=1)
    ns = (past + new_pad) // SEL_BLK
    nsp = past // SEL_BLK
    bpp = PAGE_SIZE // SEL_BLK
    nbn = new_pad // SEL_BLK
    ks_new = kv_new[:, :, 2].reshape(db, nbn, SEL_BLK, N_KV_HEADS, HEAD_DIM).transpose(0, 3, 1, 2, 4)
    vs_new = kv_new[:, :, 3].reshape(db, nbn, SEL_BLK, N_KV_HEADS, HEAD_DIM).transpose(0, 3, 1, 2, 4)
    bb = jnp.arange(db)[:, None, None, None]
    kk = jnp.arange(N_KV_HEADS)[None, None, :, None]

    def fetch(idx):
        jp = jnp.minimum(idx, nsp - 1)
        page = page_table[bb, jp // bpp][..., None]
        rows = (jp % bpp)[..., None] * SEL_BLK + jnp.arange(SEL_BLK, dtype=jnp.int32)
        ks_p = cache_kv[page, l, 2, kk[..., None], rows]
        vs_p = cache_kv[page, l, 3, kk[..., None], rows]
        jn = jnp.clip(idx - nsp, 0, nbn - 1)
        is_new = (idx >= nsp)[..., None, None]
        return jnp.where(is_new, ks_new[bb, kk, jn], ks_p), jnp.where(is_new, vs_new[bb, kk, jn], vs_p)

    wb = win.shape[3]
    kw_all = jnp.concatenate([win[:, 0].transpose(0, 2, 1, 3), kv[:, :, 4]], axis=1)
    vw_all = jnp.concatenate([win[:, 1].transpose(0, 2, 1, 3), kv[:, :, 5]], axis=1)
    pos_w = past - wb + jnp.arange(wb + ds, dtype=jnp.int32)
    pos_q = past + jnp.arange(ds, dtype=jnp.int32)
    o = nsa_core(q, pos_q, kcmp, vcmp, fetch, ns, kw_all, vw_all, pos_w, g_nsa, rel_bias)
    new_win = jnp.stack([kw_all[:, -wb:], vw_all[:, -wb:]], axis=1).transpose(0, 1, 3, 2, 4)
    return o, new_win


def conv_branch(u_in, hist, w, b, ln_g, ln_b, w_o):
    u = u_in[..., :D_CONV] * jax.nn.sigmoid(u_in[..., D_CONV:])
    y = jax.nn.silu(layernorm(causal_dwconv(hist, u, w, b), ln_g, ln_b))
    return y @ w_o, u


def merge(ya, yb, g_br, w_out):
    return (jax.nn.sigmoid(g_br[..., :D_MODEL]) * ya + jax.nn.sigmoid(g_br[..., D_MODEL:]) * yb) @ w_out


def conv_ffn(h, hist, w_up, cw, cb, w_down):
    z = h @ w_up
    zc = causal_dwconv(hist, z, cw, cb)
    return (jax.nn.gelu(zc[..., :D_FF]) * zc[..., D_FF:]) @ w_down, z


def setup_inputs(seed: int = 0) -> dict:
    key = jax.random.key(seed)
    ks = jax.random.split(key, 32)
    f32 = jnp.float32
    n_pages = PAST_LEN // PAGE_SIZE
    n_used = DEC_BATCH * n_pages
    n_pool = n_used + max(1, n_used // 4)
    w_buf = min(WINDOW, PAST_LEN)

    def nrm(k, shape, scale):
        return jax.random.normal(k, shape, f32) * scale

    page_table = jax.random.permutation(ks[0], n_pool)[:n_used].reshape(DEC_BATCH, n_pages).astype(jnp.int32)
    return {
        "x_prompt": nrm(ks[1], (BATCH, SEQ, D_MODEL), 1.0),
        "x_sample": nrm(ks[2], (DEC_BATCH, DEC_SEQ, D_MODEL), 1.0),
        "cache_kv": nrm(ks[3], (n_pool, DEPTH, 4, N_KV_HEADS, PAGE_SIZE, HEAD_DIM), 1.0),
        "state_win_kv": nrm(ks[4], (DEC_BATCH, DEPTH, 2, N_KV_HEADS, w_buf, HEAD_DIM), 1.0),
        "state_conv": nrm(ks[5], (DEC_BATCH, DEPTH, CONV_K - 1, D_CONV), 0.5),
        "state_ffn": nrm(ks[6], (DEC_BATCH, DEPTH, FFN_K - 1, 2 * D_FF), 1.0),
        "page_table": page_table,
        "c_prompt": nrm(ks[7], (BATCH, D_MODEL), 1.0),
        "c_sample": nrm(ks[8], (DEC_BATCH, D_MODEL), 1.0),
        "w_ada": nrm(ks[9], (DEPTH, D_MODEL, 6 * D_MODEL), 0.5 * D_MODEL ** -0.5),
        "b_ada": nrm(ks[10], (DEPTH, 6 * D_MODEL), 0.01),
        "g_norm1": 1.0 + nrm(ks[11], (DEPTH, D_MODEL), 0.02),
        "g_norm2": 1.0 + nrm(ks[12], (DEPTH, D_MODEL), 0.02),
        "w_in": nrm(ks[13], (DEPTH, D_MODEL, N_IN), D_MODEL ** -0.5),
        "conv_w": nrm(ks[14], (DEPTH, CONV_K, D_CONV), CONV_K ** -0.5),
        "conv_b": nrm(ks[15], (DEPTH, D_CONV), 0.01),
        "conv_ln_g": 1.0 + nrm(ks[16], (DEPTH, D_CONV), 0.02),
        "conv_ln_b": nrm(ks[17], (DEPTH, D_CONV), 0.01),
        "w_conv_out": nrm(ks[18], (DEPTH, D_CONV, D_MODEL), D_CONV ** -0.5),
        "cmp_pe": nrm(ks[19], (DEPTH, 2, CMP_BLK, HEAD_DIM), 0.1),
        "cmp_w1": nrm(ks[20], (DEPTH, 2, HEAD_DIM, HEAD_DIM), HEAD_DIM ** -0.5),
        "cmp_w2": nrm(ks[21], (DEPTH, 2, HEAD_DIM, HEAD_DIM), HEAD_DIM ** -0.5),
        "w_nsa_out": nrm(ks[22], (DEPTH, N_HEADS * HEAD_DIM, D_MODEL), (N_HEADS * HEAD_DIM) ** -0.5),
        "w_out": nrm(ks[23], (DEPTH, D_MODEL, D_MODEL), D_MODEL ** -0.5),
        "w_up": nrm(ks[24], (DEPTH, D_MODEL, 2 * D_FF), D_MODEL ** -0.5),
        "ffn_conv_w": nrm(ks[25], (DEPTH, FFN_K, 2 * D_FF), FFN_K ** -0.5),
        "ffn_conv_b": nrm(ks[26], (DEPTH, 2 * D_FF), 0.01),
        "w_down": nrm(ks[27], (DEPTH, D_FF, D_MODEL), D_FF ** -0.5),
        "rel_bias": nrm(ks[28], (N_BUCKETS, N_HEADS), 0.3),
        "g_final": 1.0 + nrm(ks[29], (D_MODEL,), 0.02),
    }


def reference(x_prompt, x_sample, cache_kv, state_win_kv, state_conv, state_ffn, page_table, c_prompt, c_sample,
              w_ada, b_ada, g_norm1, g_norm2, w_in, conv_w, conv_b, conv_ln_g, conv_ln_b, w_conv_out,
              cmp_pe, cmp_w1, cmp_w2, w_nsa_out, w_out, w_up, ffn_conv_w, ffn_conv_b, w_down, rel_bias, g_final):
    xp, xs = x_prompt, x_sample
    b, s = xp.shape[:2]
    w_len = min(WINDOW, s)
    kvp, kvs, winp, wins, convp, convs, ffnp, ffns = [], [], [], [], [], [], [], []
    for l in range(DEPTH):
        mp = adaln(c_prompt, w_ada[l], b_ada[l])
        ms = adaln(c_sample, w_ada[l], b_ada[l])
        hp = rmsnorm(xp, g_norm1[l]) * (1.0 + mp[1]) + mp[0]
        u_in, q, kv, g_nsa, g_br = split_proj(hp, w_in[l])
        ya, up = conv_branch(u_in, jnp.zeros((b, CONV_K - 1, D_CONV), xp.dtype), conv_w[l], conv_b[l],
                             conv_ln_g[l], conv_ln_b[l], w_conv_out[l])
        yb = nsa_prompt(q, kv, g_nsa, cmp_pe[l], cmp_w1[l], cmp_w2[l], rel_bias) @ w_nsa_out[l]
        xp = xp + mp[2] * merge(ya, yb, g_br, w_out[l])
        h2 = rmsnorm(xp, g_norm2[l]) * (1.0 + mp[4]) + mp[3]
        f, zp = conv_ffn(h2, jnp.zeros((b, FFN_K - 1, 2 * D_FF), xp.dtype), w_up[l], ffn_conv_w[l],
                         ffn_conv_b[l], w_down[l])
        xp = xp + mp[5] * f
        kv_t = kv.transpose(0, 2, 3, 1, 4)
        kvp.append(kv_t[:, :4])
        winp.append(kv_t[:, 4:, :, s - w_len:])
        convp.append(up[:, -(CONV_K - 1):])
        ffnp.append(zp[:, -(FFN_K - 1):])
        hs = rmsnorm(xs, g_norm1[l]) * (1.0 + ms[1]) + ms[0]
        u_in, q, kv, g_nsa, g_br = split_proj(hs, w_in[l])
        hist_c = state_conv[:, l]
        ya, us = conv_branch(u_in, hist_c, conv_w[l], conv_b[l], conv_ln_g[l], conv_ln_b[l], w_conv_out[l])
        o, new_win = nsa_sample(q, kv, g_nsa, cache_kv, l, page_table, state_win_kv[:, l],
                                cmp_pe[l], cmp_w1[l], cmp_w2[l], rel_bias)
        xs = xs + ms[2] * merge(ya, o @ w_nsa_out[l], g_br, w_out[l])
        h2 = rmsnorm(xs, g_norm2[l]) * (1.0 + ms[4]) + ms[3]
        hist_f = state_ffn[:, l]
        f, zs = conv_ffn(h2, hist_f, w_up[l], ffn_conv_w[l], ffn_conv_b[l], w_down[l])
        xs = xs + ms[5] * f
        kvs.append(kv.transpose(0, 2, 3, 1, 4)[:, :4])
        wins.append(new_win)
        convs.append(jnp.concatenate([hist_c.astype(us.dtype), us], axis=1)[:, -(CONV_K - 1):])
        ffns.append(jnp.concatenate([hist_f.astype(zs.dtype), zs], axis=1)[:, -(FFN_K - 1):])
    y_prompt = rmsnorm(xp, g_final)
    y_sample = rmsnorm(xs, g_final)
    kv_prompt = jnp.stack(kvp, axis=1)
    kv_sample = jnp.stack(kvs, axis=1)
    win_prompt = jnp.stack(winp, axis=1)
    win_sample = jnp.stack(wins, axis=1)
    conv_prompt = jnp.stack(convp, axis=1)
    conv_sample = jnp.stack(convs, axis=1)
    ffn_prompt = jnp.stack(ffnp, axis=1)
    ffn_sample = jnp.stack(ffns, axis=1)
    return (y_prompt, y_sample, kv_prompt, kv_sample, win_prompt, win_sample, conv_prompt, conv_sample, ffn_prompt, ffn_sample)
```

```python
import functools
import math

import jax
import jax.numpy as jnp
from jax import lax
from jax.experimental import pallas as pl
from jax.experimental.pallas import tpu as pltpu

F32 = jnp.float32
BF16 = jnp.bfloat16

N_HEADS = 16
HEAD_DIM = 64
N_KV_HEADS = 4
GROUP = N_HEADS // N_KV_HEADS
CMP_BLK = 32
SEL_BLK = 64
N_SEL = 16
WINDOW = 512
N_BUCKETS = 32
MAX_DISTANCE = 1024
CONV_K = 31
FFN_K = 3
PAGE_SIZE = 128
EPS = 1e-6
FORCE_SCORE = 1e4
NEG = -1e30
REMOVED = -3e38
LANE = 128
CONV_HALO = 32
VMEM_LIMIT = 56 * 1024 * 1024


def _cparams(sem):
    return pltpu.CompilerParams(dimension_semantics=sem, vmem_limit_bytes=VMEM_LIMIT)


def _dot(a, b):
    return jnp.dot(a, b, preferred_element_type=F32)


def _dot_nt(a, b):
    return lax.dot_general(a, b, (((1,), (1,)), ((), ())), preferred_element_type=F32)


def _silu(x):
    return x * jax.nn.sigmoid(x)


def _ada_kernel(c_ref, w_ref, b_ref, o_ref):
    s = _silu(c_ref[...]).astype(BF16)
    o_ref[...] = _dot(s, w_ref[...].astype(BF16)) + b_ref[...]


def _ada(c, w, b):
    m, d = c.shape
    n = w.shape[1]
    tn = 1536 if n % 1536 == 0 else n
    return pl.pallas_call(
        _ada_kernel,
        grid=(n // tn,),
        in_specs=[pl.BlockSpec((m, d), lambda j: (0, 0)),
                  pl.BlockSpec((d, tn), lambda j: (0, j)),
                  pl.BlockSpec((1, tn), lambda j: (0, j))],
        out_specs=pl.BlockSpec((m, tn), lambda j: (0, j)),
        out_shape=jax.ShapeDtypeStruct((m, n), F32),
        compiler_params=_cparams(("arbitrary",)),
        name="ada_mod",
    )(c, w, b.reshape(1, n))


def _mod_spec(mods, i, ts):
    d = mods.shape[-1]
    if mods.shape[2] == 1:
        return pl.BlockSpec((1, 1, 1, d), lambda b, s, *_: (i, b, 0, 0))
    return pl.BlockSpec((1, 1, ts, d), lambda b, s, *_: (i, b, s, 0))


def _proj_kernel(x_ref, sh_ref, sc_ref, g1_ref, wu_ref, wq_ref, wkv_ref, wg_ref, wbr_ref,
                 u_ref, q_ref, kvn_ref, g_ref, br_ref, *hm_refs, d_conv, n_kvn, head_major):
    x = x_ref[0]
    r = lax.rsqrt(jnp.mean(x * x, axis=-1, keepdims=True) + EPS)
    h = (x * r) * g1_ref[...]
    h = h * (1.0 + sc_ref[0, 0]) + sh_ref[0, 0]
    hb = h.astype(BF16)
    cw = 256
    for c in range(d_conv // cw):
        a = _dot(hb, wu_ref[:, c * cw:(c + 1) * cw])
        g = _dot(hb, wu_ref[:, d_conv + c * cw:d_conv + (c + 1) * cw])
        u_ref[0, :, c * cw:(c + 1) * cw] = a * jax.nn.sigmoid(g)
    q_ref[0] = (_dot(hb, wq_ref[...]) * (HEAD_DIM ** -0.5)).astype(BF16)
    g_ref[0] = _dot(hb, wg_ref[...])
    br_ref[0] = _dot(hb, wbr_ref[...])
    kv = _dot(hb, wkv_ref[...])
    kvn_ref[0] = kv[:, :n_kvn].astype(kvn_ref.dtype)
    if head_major:
        kv4_ref, kvw_ref = hm_refs
        for slot in range(6):
            for hh in range(N_KV_HEADS):
                c0 = _kv_col(slot, hh)
                blk = kv[:, c0:c0 + HEAD_DIM]
                if slot < 4:
                    kv4_ref[0, slot * N_KV_HEADS + hh] = blk
                else:
                    kvw_ref[0, (slot - 4) * N_KV_HEADS + hh] = blk


_ATT_SLOT_ORDER = (3, 2, 5, 4)


def _kv_col(slot, hh):
    if slot >= 2:
        return hh * 4 * HEAD_DIM + _ATT_SLOT_ORDER.index(slot) * HEAD_DIM
    return N_KV_HEADS * 4 * HEAD_DIM + hh * 2 * HEAD_DIM + slot * HEAD_DIM


def _kv_perm():
    perm = [0] * (6 * N_KV_HEADS * HEAD_DIM)
    for slot in range(6):
        for hh in range(N_KV_HEADS):
            src = (slot * N_KV_HEADS + hh) * HEAD_DIM
            dst = _kv_col(slot, hh)
            for e in range(HEAD_DIM):
                perm[dst + e] = src + e
    return jnp.asarray(perm, dtype=jnp.int32)


def _resident(shape):
    nd = len(shape)
    return pl.BlockSpec(shape, lambda *_: (0,) * nd, pipeline_mode=pl.Buffered(1))


def _proj(x, mods, g1, wu, wq, wkv, wg, wbr, *, head_major, ts):
    b, s, d = x.shape
    d_conv = wu.shape[1] // 2
    nkv = wkv.shape[1]
    n_kvn = N_KV_HEADS * 4 * HEAD_DIM if head_major else nkv
    kvn_dtype = BF16 if head_major else F32
    row = lambda n: pl.BlockSpec((1, ts, n), lambda bb, ss: (bb, ss, 0))
    out_shape = [jax.ShapeDtypeStruct((b, s, d_conv), F32),
                 jax.ShapeDtypeStruct((b, s, wq.shape[1]), BF16),
                 jax.ShapeDtypeStruct((b, s, n_kvn), kvn_dtype),
                 jax.ShapeDtypeStruct((b, s, wg.shape[1]), F32),
                 jax.ShapeDtypeStruct((b, s, wbr.shape[1]), F32)]
    out_specs = [row(d_conv), row(wq.shape[1]), row(n_kvn), row(wg.shape[1]), row(wbr.shape[1])]
    if head_major:
        out_shape += [jax.ShapeDtypeStruct((b, 4 * N_KV_HEADS, s, HEAD_DIM), F32),
                      jax.ShapeDtypeStruct((b, 2 * N_KV_HEADS, s, HEAD_DIM), F32)]
        out_specs += [pl.BlockSpec((1, 4 * N_KV_HEADS, ts, HEAD_DIM), lambda bb, ss: (bb, 0, ss, 0)),
                      pl.BlockSpec((1, 2 * N_KV_HEADS, ts, HEAD_DIM), lambda bb, ss: (bb, 0, ss, 0))]
    return pl.pallas_call(
        functools.partial(_proj_kernel, d_conv=d_conv, n_kvn=n_kvn, head_major=head_major),
        grid=(b, s // ts),
        in_specs=[row(d), _mod_spec(mods, 0, ts), _mod_spec(mods, 1, ts), _resident((1, d)),
                  _resident(wu.shape), _resident(wq.shape), _resident(wkv.shape),
                  _resident(wg.shape), _resident(wbr.shape)],
        out_specs=out_specs,
        out_shape=out_shape,
        compiler_params=_cparams(("arbitrary", "arbitrary")),
        name="in_proj",
    )(x, mods, mods, g1.reshape(1, d), wu, wq, wkv, wg, wbr)


def _ln_silu(y, g, b):
    mu = jnp.mean(y, axis=-1, keepdims=True)
    yc = y - mu
    var = jnp.mean(yc * yc, axis=-1, keepdims=True)
    return _silu(yc * lax.rsqrt(var + EPS) * g + b)


def _conv_seq_kernel(u_ref, halo_ref, w_ref, b_ref, lg_ref, lb_ref, o_ref, xs_ref, acc_ref, *, ts, rb):
    i = pl.program_id(1)
    xs_ref[0:CONV_HALO] = jnp.where(i > 0, halo_ref[0], 0.0)
    xs_ref[CONV_HALO:CONV_HALO + ts] = u_ref[0]
    c = u_ref.shape[2]
    lane = 128
    first = CONV_HALO - (CONV_K - 1)

    def col_body(ci, carry):
        c0 = pl.multiple_of(ci * lane, lane)
        for r0 in range(0, ts, rb):
            acc = jnp.zeros((rb, lane), F32)
            for k in range(CONV_K):
                acc = acc + w_ref[k:k + 1, pl.ds(c0, lane)] * xs_ref[pl.ds(r0 + first + k, rb), pl.ds(c0, lane)]
            acc_ref[pl.ds(r0, rb), pl.ds(c0, lane)] = acc
        return carry

    lax.fori_loop(0, c // lane, col_body, 0)
    o_ref[0] = _ln_silu(acc_ref[...] + b_ref[...], lg_ref[...], lb_ref[...]).astype(o_ref.dtype)


def _conv_seq(u, w, b, lg, lb, *, ts):
    bsz, s, c = u.shape
    hb = ts // CONV_HALO
    vec = lambda: pl.BlockSpec((1, c), lambda bb, ss: (0, 0))
    return pl.pallas_call(
        functools.partial(_conv_seq_kernel, ts=ts, rb=32),
        grid=(bsz, s // ts),
        in_specs=[pl.BlockSpec((1, ts, c), lambda bb, ss: (bb, ss, 0)),
                  pl.BlockSpec((1, CONV_HALO, c), lambda bb, ss: (bb, jnp.maximum(ss * hb - 1, 0), 0)),
                  pl.BlockSpec((CONV_K, c), lambda bb, ss: (0, 0)), vec(), vec(), vec()],
        out_specs=pl.BlockSpec((1, ts, c), lambda bb, ss: (bb, ss, 0)),
        out_shape=jax.ShapeDtypeStruct((bsz, s, c), BF16),
        scratch_shapes=[pltpu.VMEM((CONV_HALO + ts, c), F32), pltpu.VMEM((ts, c), F32)],
        compiler_params=_cparams(("arbitrary", "arbitrary")),
        name="conv_seq",
    )(u, u, w, b.reshape(1, c), lg.reshape(1, c), lb.reshape(1, c))


def _conv_step_kernel(hist_ref, u_ref, w_ref, b_ref, lg_ref, lb_ref, o_ref):
    hist = hist_ref[:, 0]
    y = jnp.sum(hist * w_ref[0:CONV_K - 1][None], axis=1)
    y = y + u_ref[...] * w_ref[CONV_K - 1:CONV_K] + b_ref[...]
    o_ref[...] = _ln_silu(y, lg_ref[...], lb_ref[...]).astype(o_ref.dtype)


def _conv_step(hist, u, w, b, lg, lb):
    db, c = u.shape
    gb = 16 if db % 16 == 0 else db
    vec = lambda: pl.BlockSpec((1, c), lambda i: (0, 0))
    return pl.pallas_call(
        _conv_step_kernel,
        grid=(db // gb,),
        in_specs=[pl.BlockSpec((gb, 1, CONV_K - 1, c), lambda i: (i, 0, 0, 0)),
                  pl.BlockSpec((gb, c), lambda i: (i, 0)),
                  pl.BlockSpec((CONV_K, c), lambda i: (0, 0)), vec(), vec(), vec()],
        out_specs=pl.BlockSpec((gb, c), lambda i: (i, 0)),
        out_shape=jax.ShapeDtypeStruct((db, c), BF16),
        compiler_params=_cparams(("arbitrary",)),
        name="conv_step",
    )(hist, u, w, b.reshape(1, c), lg.reshape(1, c), lb.reshape(1, c))


def _merge_kernel(ca_ref, no_ref, br_ref, x_ref, m2_ref, sh_ref, sc_ref, g2_ref,
                  wco_ref, wno_ref, wo_ref, x1_ref, h2_ref):
    d = x_ref.shape[2]
    ya = _dot(ca_ref[0], wco_ref[...])
    yb = _dot(no_ref[0], wno_ref[...])
    br = br_ref[0]
    mix = jax.nn.sigmoid(br[:, :d]) * ya + jax.nn.sigmoid(br[:, d:]) * yb
    z = _dot(mix.astype(BF16), wo_ref[...])
    x1 = x_ref[0] + m2_ref[0, 0] * z
    x1_ref[0] = x1
    r = lax.rsqrt(jnp.mean(x1 * x1, axis=-1, keepdims=True) + EPS)
    h = (x1 * r) * g2_ref[...]
    h2_ref[0] = (h * (1.0 + sc_ref[0, 0]) + sh_ref[0, 0]).astype(BF16)


def _merge(cact, nsa_o, g_br, x, mods, g2, wco, wno, wo, *, ts):
    b, s, d = x.shape
    row = lambda n: pl.BlockSpec((1, ts, n), lambda bb, ss: (bb, ss, 0))
    return pl.pallas_call(
        _merge_kernel,
        grid=(b, s // ts),
        in_specs=[row(cact.shape[2]), row(nsa_o.shape[2]), row(2 * d), row(d),
                  _mod_spec(mods, 2, ts), _mod_spec(mods, 3, ts), _mod_spec(mods, 4, ts),
                  _resident((1, d)), _resident(wco.shape), _resident(wno.shape), _resident(wo.shape)],
        out_specs=[row(d), row(d)],
        out_shape=[jax.ShapeDtypeStruct((b, s, d), F32), jax.ShapeDtypeStruct((b, s, d), BF16)],
        compiler_params=_cparams(("arbitrary", "arbitrary")),
        name="merge",
    )(cact, nsa_o, g_br, x, mods, mods, mods, g2.reshape(1, d), wco, wno, wo)


def _gelu_tanh(x):
    return 0.5 * x * (1.0 + jnp.tanh(math.sqrt(2.0 / math.pi) * (x + 0.044715 * (x * x * x))))


def _ffn_kernel(*refs, seq_mode, ts):
    if seq_mode:
        (h2_ref, x1_ref, m5_ref, gf_ref, wa_ref, wb_ref, cwa_ref, cwb_ref, cba_ref, cbb_ref, wd_ref,
         y_ref, za_ref, zb_ref, acc_ref, zs_ref, carry_ref) = refs
    else:
        (h2_ref, x1_ref, m5_ref, gf_ref, wa_ref, wb_ref, cwa_ref, cwb_ref, cba_ref, cbb_ref, wd_ref,
         h0a_ref, h0b_ref, h1a_ref, h1b_ref, y_ref, za_ref, zb_ref, acc_ref) = refs
    si = pl.program_id(1)
    f = pl.program_id(2)
    nf = pl.num_programs(2)
    h2 = h2_ref[0]
    halves = []
    for part, (w_ref, cw_ref, cb_ref, zt_ref) in enumerate(((wa_ref, cwa_ref, cba_ref, za_ref),
                                                           (wb_ref, cwb_ref, cbb_ref, zb_ref))):
        z = _dot(h2, w_ref[...])
        if seq_mode:
            slot = f * 2 + part
            prev = jnp.where(si > 0, carry_ref[slot], 0.0)
            zs_ref[0:8] = prev
            zs_ref[8:8 + ts] = z
            carry_ref[slot] = z[ts - 8:ts]
            zt_ref[0] = z[ts - 8:ts]
            zc = (cw_ref[0:1] * zs_ref[pl.ds(6, ts)] + cw_ref[1:2] * zs_ref[pl.ds(7, ts)]
                  + cw_ref[2:3] * z + cb_ref[...])
        else:
            h0_ref, h1_ref = ((h0a_ref, h1a_ref), (h0b_ref, h1b_ref))[part]
            zt_ref[0] = z
            zc = cw_ref[0:1] * h0_ref[...] + cw_ref[1:2] * h1_ref[...] + cw_ref[2:3] * z + cb_ref[...]
        halves.append(zc)
    act = (_gelu_tanh(halves[0]) * halves[1]).astype(BF16)
    contrib = _dot(act, wd_ref[...])

    @pl.when(f == 0)
    def _():
        acc_ref[...] = contrib

    @pl.when(f > 0)
    def _():
        acc_ref[...] += contrib

    @pl.when(f == nf - 1)
    def _():
        x2 = x1_ref[0] + m5_ref[0, 0] * acc_ref[...]
        r = lax.rsqrt(jnp.mean(x2 * x2, axis=-1, keepdims=True) + EPS)
        y_ref[0] = (x2 * r) * gf_ref[...]


def _ffn(h2, x1, mods, g_final, w_up, cw, cb, w_down, hist, *, ts, fc):
    b, s, d = x1.shape
    d_ff = w_down.shape[0]
    nf = d_ff // fc
    seq_mode = hist is None
    row = lambda n: pl.BlockSpec((1, ts, n), lambda bb, ss, ff: (bb, ss, 0))
    ca = lambda rows: pl.BlockSpec((rows, fc), lambda bb, ss, ff: (0, ff))
    cb_ = lambda rows: pl.BlockSpec((rows, fc), lambda bb, ss, ff: (0, nf + ff))
    in_specs = [row(d), row(d), _mod_spec(mods, 5, ts), pl.BlockSpec((1, d), lambda bb, ss, ff: (0, 0)),
                ca(d), cb_(d), ca(FFN_K), cb_(FFN_K), ca(1), cb_(1),
                pl.BlockSpec((fc, d), lambda bb, ss, ff: (ff, 0))]
    args = [h2, x1, mods, g_final.reshape(1, d), w_up, w_up, cw, cw, cb.reshape(1, -1), cb.reshape(1, -1), w_down]
    zrows = 8 * (s // ts) if seq_mode else ts
    scratch = [pltpu.VMEM((ts, d), F32)]
    if seq_mode:
        scratch += [pltpu.VMEM((ts + 8, fc), F32), pltpu.VMEM((2 * nf, 8, fc), F32)]
        zspec = pl.BlockSpec((1, 8, fc), lambda bb, ss, ff: (bb, ss, ff))
    else:
        assert b == 1 and s == ts
        h0, h1 = hist
        in_specs += [ca(ts), cb_(ts), ca(ts), cb_(ts)]
        args += [h0, h0, h1, h1]
        zspec = pl.BlockSpec((1, ts, fc), lambda bb, ss, ff: (bb, 0, ff))
    return pl.pallas_call(
        functools.partial(_ffn_kernel, seq_mode=seq_mode, ts=ts),
        grid=(b, s // ts, nf),
        in_specs=in_specs,
        out_specs=[row(d), zspec, zspec],
        out_shape=[jax.ShapeDtypeStruct((b, s, d), F32),
                   jax.ShapeDtypeStruct((b, zrows, d_ff), F32),
                   jax.ShapeDtypeStruct((b, zrows, d_ff), F32)],
        scratch_shapes=scratch,
        compiler_params=_cparams(("arbitrary", "arbitrary", "arbitrary")),
        name="ffn",
    )(*args)


def _t5_bucket(d):
    max_exact = N_BUCKETS // 2
    df = jnp.maximum(d, 1).astype(F32)
    large = max_exact + (jnp.log(df / max_exact) / math.log(MAX_DISTANCE / max_exact)
                         * (N_BUCKETS - max_exact)).astype(jnp.int32)
    large = jnp.minimum(large, N_BUCKETS - 1)
    return jnp.where(d < max_exact, d, large)


def _bias_kernel(rb_ref, o_ref, *, off, ostride, rs, cs, dmax):
    hh = pl.program_id(0)
    o = pl.program_id(1)
    rows, cols = o_ref.shape[2], o_ref.shape[3]
    d = (off + o * ostride + rs * lax.broadcasted_iota(jnp.int32, (rows, cols), 0)
         + cs * lax.broadcasted_iota(jnp.int32, (rows, cols), 1))
    bucket = _t5_bucket(jnp.maximum(d, 0))
    val = jnp.zeros((rows, cols), F32)
    for bk in range(N_BUCKETS):
        val = jnp.where(bucket == bk, rb_ref[bk, hh], val)
    ok = d >= 0
    if dmax is not None:
        ok = ok & (d <= dmax)
    o_ref[0, 0] = jnp.where(ok, val, NEG)


def _bias_table(rel_bias, n_o, rows, cols, *, off, ostride=0, rs, cs, dmax=None):
    return pl.pallas_call(
        functools.partial(_bias_kernel, off=off, ostride=ostride, rs=rs, cs=cs, dmax=dmax),
        grid=(N_HEADS, n_o),
        in_specs=[pl.BlockSpec(memory_space=pltpu.SMEM)],
        out_specs=pl.BlockSpec((1, 1, rows, cols), lambda hh, o: (hh, o, 0, 0)),
        out_shape=jax.ShapeDtypeStruct((N_HEADS, n_o, rows, cols), F32),
        compiler_params=_cparams(("arbitrary", "arbitrary")),
        name="bias_table",
    )(rel_bias)


def _compress_rows(x, pe, w1b, w2b):
    r = x.shape[0]
    xb = (x.reshape(r // CMP_BLK, CMP_BLK, HEAD_DIM) + pe[None]).reshape(r, HEAD_DIM).astype(BF16)
    h = _silu(_dot(xb, w1b))
    hm = jnp.sum(h.reshape(r // CMP_BLK, CMP_BLK, HEAD_DIM), axis=1) * (1.0 / CMP_BLK)
    return _dot(hm.astype(BF16), w2b)


def _compress_kernel(x_ref, pe_ref, w1_ref, w2_ref, o_ref):
    o_ref[0, 0] = _compress_rows(x_ref[0, 0], pe_ref[0], w1_ref[0].astype(BF16), w2_ref[0].astype(BF16))


def _compress_seq(kv4, pe, w1, w2):
    b, _, s, hd = kv4.shape
    n = 2 * N_KV_HEADS
    par = lambda: pl.BlockSpec((1,) + pe.shape[1:], lambda bb, j: (j // N_KV_HEADS, 0, 0))
    return pl.pallas_call(
        _compress_kernel,
        grid=(b, n),
        in_specs=[pl.BlockSpec((1, 1, s, hd), lambda bb, j: (bb, j, 0, 0)),
                  par(), pl.BlockSpec((1, hd, hd), lambda bb, j: (j // N_KV_HEADS, 0, 0)),
                  pl.BlockSpec((1, hd, hd), lambda bb, j: (j // N_KV_HEADS, 0, 0))],
        out_specs=pl.BlockSpec((1, 1, s // CMP_BLK, hd), lambda bb, j: (bb, j, 0, 0)),
        out_shape=jax.ShapeDtypeStruct((b, n, s // CMP_BLK, hd), F32),
        compiler_params=_cparams(("arbitrary", "arbitrary")),
        name="compress_seq",
    )(kv4, pe, w1, w2)


def _topk_axis0(score, n_top):
    ns = score.shape[0]
    js = lax.broadcasted_iota(jnp.int32, score.shape, 0)
    sel = jnp.zeros(score.shape, jnp.bool_)
    winners = []
    for _ in range(n_top):
        m = jnp.max(score, axis=0, keepdims=True)
        first = jnp.min(jnp.where(score == m, js, ns), axis=0, keepdims=True)
        hit = js == first
        sel = sel | hit
        score = jnp.where(hit, REMOVED, score)
        winners.append(first)
    return sel, winners


def _flash_step(qa, k_tile, v_tile, bias, m_ref, l_ref, acc_ref, idx):
    s = _dot_nt(qa, k_tile) + bias
    m_old = m_ref[idx]
    m_new = jnp.maximum(m_old, jnp.max(s, axis=-1, keepdims=True))
    alpha = jnp.exp(m_old - m_new)
    p = jnp.exp(s - m_new)
    l_ref[idx] = alpha * l_ref[idx] + jnp.sum(p, axis=-1, keepdims=True)
    acc_ref[idx] = alpha * acc_ref[idx] + _dot(p.astype(BF16), v_tile)
    m_ref[idx] = m_new


def _nsa_seq_kernel(q_ref, kv_ref, kc_ref, vc_ref, g_ref, bc_ref, bs_ref, bw_ref, far_ref, o_ref,
                    kaug_ref, kwaug_ref, qa_ref, impt_ref, oc_ref, m_ref, l_ref, acc_ref,
                    *, t, n_near, n_top):
    kh = pl.program_id(1)
    i = pl.program_id(2)
    s_len = kv_ref.shape[1]
    ns = s_len // SEL_BLK
    hd = HEAD_DIM

    @pl.when(i == 0)
    def _():
        blk = lax.broadcasted_iota(jnp.int32, (s_len, ns), 0) // SEL_BLK
        col = lax.broadcasted_iota(jnp.int32, (s_len, ns), 1)
        kaug_ref[:, 0:hd] = kv_ref[0, :, hd:2 * hd]
        kaug_ref[:, hd:hd + ns] = jnp.where(blk == col, NEG, 0.0).astype(BF16)
        kwaug_ref[:, 0:hd] = kv_ref[0, :, 3 * hd:4 * hd]
        kwaug_ref[:, hd:hd + ns] = jnp.zeros((s_len, ns), BF16)

    kc = kc_ref[0, 0].astype(BF16)
    vc = vc_ref[0, 0].astype(BF16)
    q4 = q_ref[0]
    imp = None
    for g in range(GROUP):
        qh = q4[:, g * hd:(g + 1) * hd]
        bias = bc_ref[g, 0]
        lc = _dot_nt(qh, kc) + bias
        mask = bias > 0.5 * NEG
        m = jnp.max(lc, axis=-1, keepdims=True)
        e = jnp.where(mask, jnp.exp(lc - m), 0.0)
        p = e / jnp.maximum(jnp.sum(e, axis=-1, keepdims=True), 1e-30)
        oc_ref[g] = _dot(p.astype(BF16), vc)
        imp = p if imp is None else imp + p
    ratio = SEL_BLK // CMP_BLK
    imp_t = imp.T
    halves = []
    for c in range(t // LANE):
        impt_ref[c] = imp_t[:, c * LANE:(c + 1) * LANE]
        part = impt_ref[c, pl.ds(0, ns, stride=ratio), :]
        for rr in range(1, ratio):
            part = part + impt_ref[c, pl.ds(rr, ns, stride=ratio), :]
        halves.append(part)
    imps = jnp.concatenate(halves, axis=1)
    js = lax.broadcasted_iota(jnp.int32, (ns, t), 0)
    pos = i * t + lax.broadcasted_iota(jnp.int32, (ns, t), 1)
    qblk = pos // SEL_BLK
    forced = (js == 0) | (js == qblk) | (js == qblk - 1)
    score = jnp.where(forced, FORCE_SCORE, jnp.where(js * SEL_BLK <= pos, imps, -1.0))
    sel, _ = _topk_axis0(score, n_top)
    notsel = jnp.where(sel, 0.0, 1.0).T.astype(BF16)
    for g in range(GROUP):
        qa_ref[g, :, 0:hd] = q4[:, g * hd:(g + 1) * hd]
        qa_ref[g, :, hd:hd + ns] = notsel

    m_ref[...] = jnp.full(m_ref.shape, NEG, F32)
    l_ref[...] = jnp.zeros(l_ref.shape, F32)
    acc_ref[...] = jnp.zeros(acc_ref.shape, F32)
    n_far = jnp.maximum(i - (n_near - 1), 0)

    def far_body(j, carry):
        r0 = pl.multiple_of(j * t, t)
        for g in range(GROUP):
            _flash_step(qa_ref[g], kaug_ref[pl.ds(r0, t), :], kv_ref[0, pl.ds(r0, t), 0:2 * hd],
                        far_ref[kh * GROUP + g], m_ref, l_ref, acc_ref, g)
        return carry

    lax.fori_loop(0, n_far, far_body, 0)

    def near_body(j, carry):
        r0 = pl.multiple_of(j * t, t)
        for g in range(GROUP):
            _flash_step(qa_ref[g], kaug_ref[pl.ds(r0, t), :], kv_ref[0, pl.ds(r0, t), 0:2 * hd],
                        bs_ref[g, i - j], m_ref, l_ref, acc_ref, g)
        return carry

    lax.fori_loop(n_far, i + 1, near_body, 0)

    n_wt = -(-WINDOW // t) + 1
    for oo in range(n_wt - 1, -1, -1):
        @pl.when(i - oo >= 0)
        def _(oo=oo):
            r0 = pl.multiple_of((i - oo) * t, t)
            for g in range(GROUP):
                bias = bw_ref[g, 0] if oo == n_wt - 1 else bs_ref[g, oo]
                _flash_step(qa_ref[g], kwaug_ref[pl.ds(r0, t), :], kv_ref[0, pl.ds(r0, t), 2 * hd:4 * hd],
                            bias, m_ref, l_ref, acc_ref, GROUP + g)

    gates = jax.nn.sigmoid(g_ref[0])
    for g in range(GROUP):
        hh = kh * GROUP + g
        col = lax.broadcasted_iota(jnp.int32, gates.shape, 1)
        gate = lambda r: jnp.sum(jnp.where(col == hh * 3 + r, gates, 0.0), axis=-1, keepdims=True)
        o_s = acc_ref[g][:, 0:hd] / l_ref[g]
        o_w = acc_ref[GROUP + g][:, 0:hd] / l_ref[GROUP + g]
        o = gate(0) * oc_ref[g] + gate(1) * o_s + gate(2) * o_w
        o_ref[0, :, g * hd:(g + 1) * hd] = o.astype(o_ref.dtype)


def _nsa_seq(q, kvn, cmp, g_nsa, rel_bias, *, t):
    b, s, _ = q.shape
    hd = HEAD_DIM
    ns = s // SEL_BLK
    nc = s // CMP_BLK
    n_top = min(N_SEL, ns)
    assert ns <= hd and s % t == 0 and t % SEL_BLK == 0
    n_near = min(-(-(MAX_DISTANCE + t - 1) // t), s // t)
    n_wt = -(-WINDOW // t) + 1
    assert n_wt <= n_near + 1
    bias_s = _bias_table(rel_bias, n_near, t, t, off=0, ostride=t, rs=1, cs=-1)
    bias_w = _bias_table(rel_bias, 1, t, t, off=(n_wt - 1) * t, rs=1, cs=-1, dmax=WINDOW)
    bias_c = _bias_table(rel_bias, s // t, t, nc, off=-(CMP_BLK - 1), ostride=t, rs=1, cs=-CMP_BLK)
    far = rel_bias[N_BUCKETS - 1]
    gw = 4 * hd
    return pl.pallas_call(
        functools.partial(_nsa_seq_kernel, t=t, n_near=n_near, n_top=n_top),
        grid=(b, N_KV_HEADS, s // t),
        in_specs=[pl.BlockSpec((1, t, gw), lambda bb, k, i: (bb, i, k)),
                  pl.BlockSpec((1, s, gw), lambda bb, k, i: (bb, 0, k)),
                  pl.BlockSpec((1, 1, nc, hd), lambda bb, k, i: (bb, k, 0, 0)),
                  pl.BlockSpec((1, 1, nc, hd), lambda bb, k, i: (bb, N_KV_HEADS + k, 0, 0)),
                  pl.BlockSpec((1, t, g_nsa.shape[2]), lambda bb, k, i: (bb, i, 0)),
                  pl.BlockSpec((GROUP, 1, t, nc), lambda bb, k, i: (k, i, 0, 0)),
                  pl.BlockSpec((GROUP, n_near, t, t), lambda bb, k, i: (k, 0, 0, 0)),
                  pl.BlockSpec((GROUP, 1, t, t), lambda bb, k, i: (k, 0, 0, 0)),
                  pl.BlockSpec(memory_space=pltpu.SMEM)],
        out_specs=pl.BlockSpec((1, t, gw), lambda bb, k, i: (bb, i, k)),
        out_shape=jax.ShapeDtypeStruct((b, s, N_HEADS * hd), BF16),
        scratch_shapes=[pltpu.VMEM((s, 2 * hd), BF16), pltpu.VMEM((s, 2 * hd), BF16),
                        pltpu.VMEM((GROUP, t, 2 * hd), BF16), pltpu.VMEM((t // LANE, nc, LANE), F32),
                        pltpu.VMEM((GROUP, t, hd), F32),
                        pltpu.VMEM((2 * GROUP, t, 1), F32), pltpu.VMEM((2 * GROUP, t, 1), F32),
                        pltpu.VMEM((2 * GROUP, t, 2 * hd), F32)],
        compiler_params=_cparams(("arbitrary", "arbitrary", "arbitrary")),
        name="nsa_seq",
    )(q, kvn, cmp, cmp, g_nsa, bias_c, bias_s, bias_w, far)


def _compress_pages_kernel(pt_ref, *refs, pg):
    page_refs = refs[:pg]
    pe_ref, w1_ref, w2_ref, o_ref = refs[pg:]
    bpp = PAGE_SIZE // CMP_BLK
    for slot in range(2):
        x = jnp.concatenate([page_refs[n][0, 0, slot].reshape(N_KV_HEADS * PAGE_SIZE, HEAD_DIM)
                             for n in range(pg)], axis=0)
        cm = _compress_rows(x, pe_ref[slot], w1_ref[slot].astype(BF16), w2_ref[slot].astype(BF16))
        for n in range(pg):
            for hh in range(N_KV_HEADS):
                r0 = (n * N_KV_HEADS + hh) * bpp
                o_ref[0, slot, hh, n * bpp:(n + 1) * bpp, :] = cm[r0:r0 + bpp]


def _compress_pages(cache_kv, page_table, layer, pe, w1, w2, *, pg):
    db, n_pages = page_table.shape
    bpp = PAGE_SIZE // CMP_BLK
    hd = HEAD_DIM

    def page_map(b, p, pt, n):
        return (pt[b * n_pages + p * pg + n], layer, 0, 0, 0, 0)

    full = lambda a: pl.BlockSpec(a.shape, lambda b, p, pt: (0,) * a.ndim)
    grid_spec = pltpu.PrefetchScalarGridSpec(
        num_scalar_prefetch=1,
        grid=(db, n_pages // pg),
        in_specs=[pl.BlockSpec((1, 1, 2, N_KV_HEADS, PAGE_SIZE, hd), functools.partial(page_map, n=n))
                  for n in range(pg)] + [full(pe), full(w1), full(w2)],
        out_specs=pl.BlockSpec((1, 2, N_KV_HEADS, pg * bpp, hd), lambda b, p, pt: (b, 0, 0, p, 0)),
    )
    return pl.pallas_call(
        functools.partial(_compress_pages_kernel, pg=pg),
        grid_spec=grid_spec,
        out_shape=jax.ShapeDtypeStruct((db, 2, N_KV_HEADS, n_pages * bpp, hd), F32),
        compiler_params=_cparams(("arbitrary", "arbitrary")),
        name="compress_pages",
    )(page_table.reshape(-1), *([cache_kv] * pg), pe, w1, w2)


def _nsa_step_cmp_kernel(q_ref, past_ref, new_ref, pe_ref, w1_ref, w2_ref, bcp_ref, bcn_ref,
                         oc_ref, idx_ref, kn_ref, imp_ref, impt_ref, *, gb, past_len, n_top):
    hd = HEAD_DIM
    ncp = past_ref.shape[3]
    nnew = bcn_ref.shape[1]
    ratio = SEL_BLK // CMP_BLK
    new_pad = SEL_BLK
    rows = gb * N_KV_HEADS * new_pad
    cm_new = []
    for slot in range(2):
        r_in_blk = lax.broadcasted_iota(jnp.int32, (gb * N_KV_HEADS, new_pad, hd), 1)
        xnew = new_ref[:, slot * N_KV_HEADS:(slot + 1) * N_KV_HEADS, :].reshape(gb * N_KV_HEADS, 1, hd)
        x = jnp.where(r_in_blk == 0, xnew, 0.0).reshape(rows, hd)
        cm_new.append(_compress_rows(x, pe_ref[slot], w1_ref[slot].astype(BF16), w2_ref[slot].astype(BF16)))
    nb_new = new_pad // CMP_BLK
    kn_ref[...] = jnp.zeros(kn_ref.shape, F32)
    imp_ref[...] = jnp.zeros(imp_ref.shape, F32)
    for bi in range(gb):
        for k in range(N_KV_HEADS):
            r0 = (bi * N_KV_HEADS + k) * nb_new
            kn_ref[0, 0:nb_new] = cm_new[0][r0:r0 + nb_new]
            kn_ref[1, 0:nb_new] = cm_new[1][r0:r0 + nb_new]
            qg = q_ref[bi, k * GROUP:(k + 1) * GROUP, :]
            bias = jnp.concatenate([bcp_ref[k * GROUP:(k + 1) * GROUP, :],
                                    bcn_ref[k * GROUP:(k + 1) * GROUP, :]], axis=1)
            lc = jnp.concatenate([_dot_nt(qg, past_ref[bi, 0, k].astype(BF16)),
                                  _dot_nt(qg, kn_ref[0].astype(BF16))], axis=1) + bias
            mask = bias > 0.5 * NEG
            m = jnp.max(lc, axis=-1, keepdims=True)
            e = jnp.where(mask, jnp.exp(lc - m), 0.0)
            p = e / jnp.maximum(jnp.sum(e, axis=-1, keepdims=True), 1e-30)
            pb = p.astype(BF16)
            oc_ref[bi, k * GROUP:(k + 1) * GROUP, :] = (_dot(pb[:, :ncp], past_ref[bi, 1, k].astype(BF16))
                                                        + _dot(pb[:, ncp:], kn_ref[1].astype(BF16)))
            row = bi * N_KV_HEADS + k
            imp_ref[row:row + 1, :] = jnp.sum(p, axis=0, keepdims=True)
    impt_ref[...] = imp_ref[...].T
    nsr = (ncp + nnew) // ratio
    imps = impt_ref[pl.ds(0, nsr, stride=ratio), :]
    for rr in range(1, ratio):
        imps = imps + impt_ref[pl.ds(rr, nsr, stride=ratio), :]
    ns = (past_len + new_pad) // SEL_BLK
    js = lax.broadcasted_iota(jnp.int32, imps.shape, 0)
    qblk = past_len // SEL_BLK
    forced = (js == 0) | (js == qblk) | (js == qblk - 1)
    score = jnp.where(forced, FORCE_SCORE, jnp.where(js * SEL_BLK <= past_len, imps, -1.0))
    score = jnp.where(js < ns, score, REMOVED)
    _, winners = _topk_axis0(score, n_top)
    idx_ref[0] = jnp.concatenate(winners, axis=0)


def _nsa_step_cmp(q, cmp_past, kv_new01, pe, w1, w2, bias_cp, bias_cn, *, gb, past_len):
    db = q.shape[0]
    ncp = cmp_past.shape[3]
    nnew = bias_cn.shape[1]
    hd = HEAD_DIM
    ns = (past_len + SEL_BLK) // SEL_BLK
    n_top = min(N_SEL, ns)
    assert gb * N_KV_HEADS <= LANE
    full = lambda a: pl.BlockSpec(a.shape, lambda i: (0,) * a.ndim)
    return pl.pallas_call(
        functools.partial(_nsa_step_cmp_kernel, gb=gb, past_len=past_len, n_top=n_top),
        grid=(db // gb,),
        in_specs=[pl.BlockSpec((gb, N_HEADS, hd), lambda i: (i, 0, 0)),
                  pl.BlockSpec((gb, 2, N_KV_HEADS, ncp, hd), lambda i: (i, 0, 0, 0, 0)),
                  pl.BlockSpec((gb, 2 * N_KV_HEADS, hd), lambda i: (i, 0, 0)),
                  full(pe), full(w1), full(w2), full(bias_cp), full(bias_cn)],
        out_specs=[pl.BlockSpec((gb, N_HEADS, hd), lambda i: (i, 0, 0)),
                   pl.BlockSpec((1, n_top, LANE), lambda i: (i, 0, 0))],
        out_shape=[jax.ShapeDtypeStruct((db, N_HEADS, hd), F32),
                   jax.ShapeDtypeStruct((db // gb, n_top, LANE), jnp.int32)],
        scratch_shapes=[pltpu.VMEM((2, nnew, hd), F32), pltpu.VMEM((LANE, ncp + nnew), F32),
                        pltpu.VMEM((ncp + nnew, LANE), F32)],
        compiler_params=_cparams(("arbitrary",)),
        name="nsa_step_cmp",
    )(q, cmp_past, kv_new01, pe, w1, w2, bias_cp, bias_cn)


def _nsa_step_sel_kernel(idx_ref, pt_ref, *refs, n_top, nsp):
    blk_refs = refs[:n_top]
    (q_ref, win_ref, new_ref, oc_ref, g_ref, bsel_ref, bwin_ref, o_ref, ks_ref, vs_ref, bs_ref) = refs[n_top:]
    b = pl.program_id(0)
    k = pl.program_id(1)
    hd = HEAD_DIM
    qg = q_ref[0, 0]
    new = new_ref[0, 0]
    row0 = lax.broadcasted_iota(jnp.int32, (SEL_BLK, hd), 0) == 0
    k_newblk = jnp.where(row0, new[2:3], 0.0)
    v_newblk = jnp.where(row0, new[3:4], 0.0)
    for n in range(n_top):
        idn = idx_ref[(b * N_KV_HEADS + k) * n_top + n]
        is_new = idn >= nsp
        ks_ref[n * SEL_BLK:(n + 1) * SEL_BLK, :] = jnp.where(is_new, k_newblk, blk_refs[n][0, 0, 0, 0]).astype(BF16)
        vs_ref[n * SEL_BLK:(n + 1) * SEL_BLK, :] = jnp.where(is_new, v_newblk, blk_refs[n][0, 0, 1, 0]).astype(BF16)
        for g in range(GROUP):
            bs_ref[g:g + 1, n * SEL_BLK:(n + 1) * SEL_BLK] = bsel_ref[g, 0, pl.ds(idn, 1), :]
    bias = bs_ref[...]
    ls = _dot_nt(qg, ks_ref[...]) + bias
    mask = bias > 0.5 * NEG
    m = jnp.max(ls, axis=-1, keepdims=True)
    e = jnp.where(mask, jnp.exp(ls - m), 0.0)
    p = e / jnp.maximum(jnp.sum(e, axis=-1, keepdims=True), 1e-30)
    o_s = _dot(p.astype(BF16), vs_ref[...])
    wb = win_ref.shape[4]
    bw = bwin_ref[0]
    bias_p = bw[:, 0:wb]
    bias_n = bw[:, wb:wb + 1]
    kn = new[4:5].astype(BF16).astype(F32)
    vn = new[5:6].astype(BF16).astype(F32)
    lw = _dot_nt(qg, win_ref[0, 0, 0, 0].astype(BF16)) + bias_p
    lwn = jnp.sum(qg.astype(F32) * kn, axis=-1, keepdims=True) + bias_n
    mask_p = bias_p > 0.5 * NEG
    mask_n = bias_n > 0.5 * NEG
    m = jnp.maximum(jnp.max(lw, axis=-1, keepdims=True), lwn)
    e_p = jnp.where(mask_p, jnp.exp(lw - m), 0.0)
    e_n = jnp.where(mask_n, jnp.exp(lwn - m), 0.0)
    den = jnp.maximum(jnp.sum(e_p, axis=-1, keepdims=True) + e_n, 1e-30)
    o_w = (_dot((e_p / den).astype(BF16), win_ref[0, 0, 1, 0].astype(BF16))
           + (e_n / den).astype(BF16).astype(F32) * vn)
    gates = jnp.broadcast_to(jax.nn.sigmoid(g_ref[0]), (GROUP, g_ref.shape[2]))
    col = lax.broadcasted_iota(jnp.int32, gates.shape, 1)
    head = k * GROUP + lax.broadcasted_iota(jnp.int32, gates.shape, 0)
    gate = lambda r: jnp.sum(jnp.where(col == head * 3 + r, gates, 0.0), axis=-1, keepdims=True)
    o_ref[0, 0] = gate(0) * oc_ref[0, 0] + gate(1) * o_s + gate(2) * o_w


def _nsa_step_sel(idx, page_table, cache_kv, layer, q, win, kv_new, o_c, g_nsa, bias_sel, bias_win, *, past_len):
    db, n_pages = page_table.shape
    n_top = idx.shape[-1]
    hd = HEAD_DIM
    nsp = past_len // SEL_BLK
    bpp = PAGE_SIZE // SEL_BLK
    wb = win.shape[4]

    def blk_map(b, k, ix, pt, n):
        jp = jnp.minimum(ix[(b * N_KV_HEADS + k) * n_top + n], nsp - 1)
        return (pt[b * n_pages + jp // bpp], layer, 1, k, jp % bpp, 0)

    grid_spec = pltpu.PrefetchScalarGridSpec(
        num_scalar_prefetch=2,
        grid=(db, N_KV_HEADS),
        in_specs=[pl.BlockSpec((1, 1, 2, 1, SEL_BLK, hd), functools.partial(blk_map, n=n)) for n in range(n_top)]
        + [pl.BlockSpec((1, 1, GROUP, hd), lambda b, k, ix, pt: (b, k, 0, 0)),
           pl.BlockSpec((1, 1, 2, 1, wb, hd), lambda b, k, ix, pt: (b, layer, 0, k, 0, 0)),
           pl.BlockSpec((1, 1, 6, hd), lambda b, k, ix, pt: (b, k, 0, 0)),
           pl.BlockSpec((1, 1, GROUP, hd), lambda b, k, ix, pt: (b, k, 0, 0)),
           pl.BlockSpec((1, 1, g_nsa.shape[-1]), lambda b, k, ix, pt: (b, 0, 0)),
           pl.BlockSpec((GROUP, 1) + bias_sel.shape[2:], lambda b, k, ix, pt: (k, 0, 0, 0)),
           pl.BlockSpec((1, GROUP, bias_win.shape[-1]), lambda b, k, ix, pt: (k, 0, 0))],
        out_specs=pl.BlockSpec((1, 1, GROUP, hd), lambda b, k, ix, pt: (b, k, 0, 0)),
        scratch_shapes=[pltpu.VMEM((n_top * SEL_BLK, hd), BF16), pltpu.VMEM((n_top * SEL_BLK, hd), BF16),
                        pltpu.VMEM((GROUP, n_top * SEL_BLK), F32)],
    )
    return pl.pallas_call(
        functools.partial(_nsa_step_sel_kernel, n_top=n_top, nsp=nsp),
        grid_spec=grid_spec,
        out_shape=jax.ShapeDtypeStruct((db, N_KV_HEADS, GROUP, hd), F32),
        compiler_params=_cparams(("arbitrary", "arbitrary")),
        name="nsa_step_sel",
    )(idx.reshape(-1), page_table.reshape(-1), *([cache_kv] * n_top), q, win, kv_new, o_c, g_nsa,
      bias_sel, bias_win)


def _pick(n, pref):
    for c in pref:
        if n % c == 0:
            return c
    return n


def kernel(x_prompt, x_sample, cache_kv, state_win_kv, state_conv, state_ffn, page_table, c_prompt, c_sample,
           w_ada, b_ada, g_norm1, g_norm2, w_in, conv_w, conv_b, conv_ln_g, conv_ln_b, w_conv_out,
           cmp_pe, cmp_w1, cmp_w2, w_nsa_out, w_out, w_up, ffn_conv_w, ffn_conv_b, w_down, rel_bias, g_final):
    depth = w_ada.shape[0]
    assert depth == 1
    layer = 0
    b, s, d = x_prompt.shape
    db = x_sample.shape[0]
    assert x_sample.shape[1] == 1
    hd = HEAD_DIM
    d_conv = conv_w.shape[2]
    d_ff = w_down.shape[1]
    nq = N_HEADS * hd
    nkv = 6 * N_KV_HEADS * hd
    past_len = page_table.shape[1] * PAGE_SIZE
    wb = state_win_kv.shape[4]

    o0, o1, o2, o3 = 2 * d_conv, 2 * d_conv + nq, 2 * d_conv + nq + nkv, 2 * d_conv + nq + nkv + 3 * N_HEADS
    wi = w_in[layer]
    wu = wi[:, :o0].astype(BF16)
    wq = wi[:, o0:o1].astype(BF16)
    wkv = wi[:, o1:o2].astype(BF16)
    wkv_p = wkv[:, _kv_perm()]
    wg = jnp.pad(wi[:, o2:o3], ((0, 0), (0, 128 - 3 * N_HEADS))).astype(BF16)
    wbr = wi[:, o3:].astype(BF16)
    wco = w_conv_out[layer].astype(BF16)
    wno = w_nsa_out[layer].astype(BF16)
    wo = w_out[layer].astype(BF16)
    wup = w_up[layer].astype(BF16)
    wdn = w_down[layer].astype(BF16)

    mod = _ada(jnp.concatenate([c_prompt, c_sample], axis=0), w_ada[layer], b_ada[layer])
    mod = mod.reshape(b + db, 6, d)
    mods_p = mod[:b].transpose(1, 0, 2).reshape(6, b, 1, d)
    mods_s = mod[b:].transpose(1, 0, 2).reshape(6, 1, db, d)

    ts = _pick(s, (256, 128))
    u_p, q_p, kvn_p, g_p, br_p, kv4_p, kvw_p = _proj(x_prompt, mods_p, g_norm1[layer], wu, wq, wkv_p, wg, wbr,
                                                     head_major=True, ts=ts)
    cact_p = _conv_seq(u_p, conv_w[layer], conv_b[layer], conv_ln_g[layer], conv_ln_b[layer], ts=ts)
    cmp_p = _compress_seq(kv4_p, cmp_pe[layer], cmp_w1[layer], cmp_w2[layer])
    o_p = _nsa_seq(q_p, kvn_p, cmp_p, g_p, rel_bias, t=256)
    x1_p, h2_p = _merge(cact_p, o_p, br_p, x_prompt, mods_p, g_norm2[layer], wco, wno, wo, ts=_pick(s, (512, 256)))
    y_prompt, zta, ztb = _ffn(h2_p, x1_p, mods_p, g_final, wup, ffn_conv_w[layer], ffn_conv_b[layer], wdn, None,
                              ts=_pick(s, (512, 256)), fc=512)
    w_len = min(WINDOW, s)
    kv_prompt = kv4_p.reshape(b, 1, 4, N_KV_HEADS, s, hd)
    win_prompt = kvw_p.reshape(b, 1, 2, N_KV_HEADS, s, hd)[:, :, :, :, s - w_len:]
    conv_prompt = u_p[:, None, s - (CONV_K - 1):]
    ffn_prompt = jnp.concatenate([zta, ztb], axis=-1)[:, None, zta.shape[1] - (FFN_K - 1):]

    xs = x_sample.reshape(1, db, d)
    u_s, q_s, kv_s, g_s, br_s = _proj(xs, mods_s, g_norm1[layer], wu, wq, wkv, wg, wbr, head_major=False, ts=db)
    kv_s = kv_s.reshape(db, 6, N_KV_HEADS, hd)
    cact_s = _conv_step(state_conv[:, layer:layer + 1], u_s[0], conv_w[layer], conv_b[layer],
                        conv_ln_g[layer], conv_ln_b[layer])
    cmp_past = _compress_pages(cache_kv, page_table, layer, cmp_pe[layer], cmp_w1[layer], cmp_w2[layer],
                               pg=_pick(page_table.shape[1], (8, 4, 2, 1)))
    ncp = cmp_past.shape[3]
    nnew = 128
    bias_cp = _bias_table(rel_bias, 1, 8, ncp, off=past_len - (CMP_BLK - 1), rs=0, cs=-CMP_BLK)[:, 0, 0]
    bias_cn = _bias_table(rel_bias, 1, 8, nnew, off=past_len - (CMP_BLK - 1) - ncp * CMP_BLK, rs=0, cs=-CMP_BLK)[:, 0, 0]
    gb = _pick(db, (8,))
    oc_s, idx_t = _nsa_step_cmp(q_s.reshape(db, N_HEADS, hd), cmp_past, kv_s[:, :2].reshape(db, 2 * N_KV_HEADS, hd),
                                cmp_pe[layer], cmp_w1[layer], cmp_w2[layer], bias_cp, bias_cn, gb=gb,
                                past_len=past_len)
    n_top = idx_t.shape[1]
    idx = idx_t[:, :, :gb * N_KV_HEADS].reshape(db // gb, n_top, gb, N_KV_HEADS).transpose(0, 2, 3, 1).reshape(db, N_KV_HEADS, n_top)
    ns_tot = (past_len + SEL_BLK) // SEL_BLK
    ns_rows = -(-ns_tot // 8) * 8
    bias_sel = _bias_table(rel_bias, 1, ns_rows, SEL_BLK, off=past_len, rs=-SEL_BLK, cs=-1)
    wcols = -(-(wb + 1) // 128) * 128
    bias_win = _bias_table(rel_bias, 1, 8, wcols, off=wb, rs=0, cs=-1, dmax=WINDOW)[:, 0, 0]
    bias_win = bias_win.reshape(N_KV_HEADS, GROUP, wcols)
    o_s = _nsa_step_sel(idx, page_table, cache_kv, layer, q_s.reshape(db, N_KV_HEADS, GROUP, hd), state_win_kv,
                        kv_s.transpose(0, 2, 1, 3), oc_s.reshape(db, N_KV_HEADS, GROUP, hd),
                        g_s.reshape(db, 1, -1), bias_sel, bias_win, past_len=past_len)
    o_s = o_s.reshape(1, db, nq).astype(BF16)
    x1_s, h2_s = _merge(cact_s.reshape(1, db, d_conv), o_s, br_s, xs, mods_s, g_norm2[layer], wco, wno, wo, ts=db)
    hist_f = state_ffn[:, layer]
    y_s, za_s, zb_s = _ffn(h2_s, x1_s, mods_s, g_final, wup, ffn_conv_w[layer], ffn_conv_b[layer], wdn,
                           (hist_f[:, 0], hist_f[:, 1]), ts=db, fc=512)
    y_sample = y_s.reshape(db, 1, d)
    kv_sample = kv_s[:, :4].reshape(db, 1, 4, N_KV_HEADS, 1, hd)
    win_sample = jnp.concatenate([state_win_kv[:, layer, :, :, 1:], kv_s[:, 4:, :, None, :]], axis=3)[:, None]
    conv_sample = jnp.concatenate([state_conv[:, layer, 1:], u_s[0][:, None, :]], axis=1)[:, None]
    z_s = jnp.concatenate([za_s[0], zb_s[0]], axis=-1)
    ffn_sample = jnp.concatenate([hist_f[:, 1:], z_s[:, None, :]], axis=1)[:, None]
    return (y_prompt, y_sample, kv_prompt, kv_sample, win_prompt, win_sample, conv_prompt, conv_sample,
            ffn_prompt, ffn_sample)
```

```python
import functools
import math

import jax
import jax.numpy as jnp
from jax import lax
from jax.experimental import pallas as pl
from jax.experimental.pallas import tpu as pltpu

F32 = jnp.float32
BF16 = jnp.bfloat16

N_HEADS = 16
HEAD_DIM = 64
N_KV_HEADS = 4
GROUP = N_HEADS // N_KV_HEADS
CMP_BLK = 32
SEL_BLK = 64
N_SEL = 16
WINDOW = 512
N_BUCKETS = 32
MAX_DISTANCE = 1024
CONV_K = 31
FFN_K = 3
PAGE_SIZE = 128
EPS = 1e-6
FORCE_SCORE = 1e4
NEG = -1e30
REMOVED = -3e38
LANE = 128
CONV_HALO = 32
VMEM_LIMIT = 56 * 1024 * 1024


def _cparams(sem):
    return pltpu.CompilerParams(dimension_semantics=sem, vmem_limit_bytes=VMEM_LIMIT)


def _dot(a, b):
    return jnp.dot(a, b, preferred_element_type=F32)


def _dot_nt(a, b):
    return lax.dot_general(a, b, (((1,), (1,)), ((), ())), preferred_element_type=F32)


def _silu(x):
    return x * jax.nn.sigmoid(x)


def _ada_kernel(c_ref, w_ref, b_ref, o_ref):
    s = _silu(c_ref[...]).astype(BF16)
    o_ref[...] = _dot(s, w_ref[...].astype(BF16)) + b_ref[...]


def _ada(c, w, b):
    m, d = c.shape
    n = w.shape[1]
    tn = 1536 if n % 1536 == 0 else n
    return pl.pallas_call(
        _ada_kernel,
        grid=(n // tn,),
        in_specs=[pl.BlockSpec((m, d), lambda j: (0, 0)),
                  pl.BlockSpec((d, tn), lambda j: (0, j)),
                  pl.BlockSpec((1, tn), lambda j: (0, j))],
        out_specs=pl.BlockSpec((m, tn), lambda j: (0, j)),
        out_shape=jax.ShapeDtypeStruct((m, n), F32),
        compiler_params=_cparams(("arbitrary",)),
        name="ada_mod",
    )(c, w, b.reshape(1, n))


def _mod_spec(mods, i, ts):
    d = mods.shape[-1]
    if mods.shape[2] == 1:
        return pl.BlockSpec((1, 1, 1, d), lambda b, s, *_: (i, b, 0, 0))
    return pl.BlockSpec((1, 1, ts, d), lambda b, s, *_: (i, b, s, 0))


def _norm_mod(x_ref, sh_ref, sc_ref, g_ref):
    x = x_ref[0]
    r = lax.rsqrt(jnp.mean(x * x, axis=-1, keepdims=True) + EPS)
    h = (x * r) * g_ref[...]
    return (h * (1.0 + sc_ref[0, 0]) + sh_ref[0, 0]).astype(BF16)


def _glu(hb, wu_ref, u_ref, d_conv):
    cw = 256
    for c in range(d_conv // cw):
        a = _dot(hb, wu_ref[:, c * cw:(c + 1) * cw])
        g = _dot(hb, wu_ref[:, d_conv + c * cw:d_conv + (c + 1) * cw])
        u_ref[0, :, c * cw:(c + 1) * cw] = a * jax.nn.sigmoid(g)


def _proj_step_kernel(x_ref, sh_ref, sc_ref, g1_ref, wu_ref, wq_ref, wkv_ref, wg_ref, wbr_ref,
                      u_ref, q_ref, kv_ref, g_ref, br_ref, *, d_conv):
    hb = _norm_mod(x_ref, sh_ref, sc_ref, g1_ref)
    _glu(hb, wu_ref, u_ref, d_conv)
    q_ref[0] = (_dot(hb, wq_ref[...]) * (HEAD_DIM ** -0.5)).astype(BF16)
    g_ref[0] = _dot(hb, wg_ref[...])
    br_ref[0] = _dot(hb, wbr_ref[...])
    kv_ref[0] = _dot(hb, wkv_ref[...])


def _resident(shape):
    nd = len(shape)
    return pl.BlockSpec(shape, lambda *_: (0,) * nd, pipeline_mode=pl.Buffered(1))


def _proj_step(x, mods, g1, wu, wq, wkv, wg, wbr, *, ts):
    b, s, d = x.shape
    d_conv = wu.shape[1] // 2
    row = lambda n: pl.BlockSpec((1, ts, n), lambda bb, ss: (bb, ss, 0))
    widths = (d_conv, wq.shape[1], wkv.shape[1], wg.shape[1], wbr.shape[1])
    dtypes = (F32, BF16, F32, F32, F32)
    return pl.pallas_call(
        functools.partial(_proj_step_kernel, d_conv=d_conv),
        grid=(b, s // ts),
        in_specs=[row(d), _mod_spec(mods, 0, ts), _mod_spec(mods, 1, ts), _resident((1, d)),
                  _resident(wu.shape), _resident(wq.shape), _resident(wkv.shape),
                  _resident(wg.shape), _resident(wbr.shape)],
        out_specs=[row(n) for n in widths],
        out_shape=[jax.ShapeDtypeStruct((b, s, n), dt) for n, dt in zip(widths, dtypes)],
        compiler_params=_cparams(("arbitrary", "arbitrary")),
        name="in_proj_step",
    )(x, mods, mods, g1.reshape(1, d), wu, wq, wkv, wg, wbr)


def _proj_seq_kernel(x_ref, sh_ref, sc_ref, g1_ref, wu_ref, wqt_ref, wkvt_ref, wkn_ref, wgt_ref, wbr_ref,
                     u_ref, qt_ref, kn_ref, vt_ref, kv4t_ref, kvwt_ref, gt_ref, br_ref, *, d_conv):
    hb = _norm_mod(x_ref, sh_ref, sc_ref, g1_ref)
    _glu(hb, wu_ref, u_ref, d_conv)
    qt_ref[0] = (_dot_nt(wqt_ref[...], hb) * (HEAD_DIM ** -0.5)).astype(BF16)
    gt_ref[0] = _dot_nt(wgt_ref[...], hb)
    br_ref[0] = _dot(hb, wbr_ref[...])
    kn_ref[0] = _dot(hb, wkn_ref[...]).astype(BF16)
    kvt = _dot_nt(wkvt_ref[...], hb)
    hd = HEAD_DIM
    n4 = 4 * N_KV_HEADS * hd
    kv4t_ref[0] = kvt[:n4]
    kvwt_ref[0] = kvt[n4:]
    for hh in range(N_KV_HEADS):
        vsel = (3 * N_KV_HEADS + hh) * hd
        vwin = (5 * N_KV_HEADS + hh) * hd
        vt_ref[0, hh * 2 * hd:hh * 2 * hd + hd] = kvt[vsel:vsel + hd].astype(BF16)
        vt_ref[0, hh * 2 * hd + hd:(hh + 1) * 2 * hd] = kvt[vwin:vwin + hd].astype(BF16)


def _proj_seq(x, mods, g1, wu, wqt, wkvt, wkn, wgt, wbr, *, ts):
    b, s, d = x.shape
    d_conv = wu.shape[1] // 2
    hd = HEAD_DIM
    row = lambda n: pl.BlockSpec((1, ts, n), lambda bb, ss: (bb, ss, 0))
    col = lambda n: pl.BlockSpec((1, n, ts), lambda bb, ss: (bb, 0, ss))
    nq, ng = wqt.shape[0], wgt.shape[0]
    n4, nw, nv = 4 * N_KV_HEADS * hd, 2 * N_KV_HEADS * hd, 2 * N_KV_HEADS * hd
    return pl.pallas_call(
        functools.partial(_proj_seq_kernel, d_conv=d_conv),
        grid=(b, s // ts),
        in_specs=[row(d), _mod_spec(mods, 0, ts), _mod_spec(mods, 1, ts), _resident((1, d)),
                  _resident(wu.shape), _resident(wqt.shape), _resident(wkvt.shape), _resident(wkn.shape),
                  _resident(wgt.shape), _resident(wbr.shape)],
        out_specs=[row(d_conv), col(nq), row(wkn.shape[1]), col(nv), col(n4), col(nw), col(ng), row(wbr.shape[1])],
        out_shape=[jax.ShapeDtypeStruct((b, s, d_conv), F32),
                   jax.ShapeDtypeStruct((b, nq, s), BF16),
                   jax.ShapeDtypeStruct((b, s, wkn.shape[1]), BF16),
                   jax.ShapeDtypeStruct((b, nv, s), BF16),
                   jax.ShapeDtypeStruct((b, n4, s), F32),
                   jax.ShapeDtypeStruct((b, nw, s), F32),
                   jax.ShapeDtypeStruct((b, ng, s), F32),
                   jax.ShapeDtypeStruct((b, s, wbr.shape[1]), F32)],
        compiler_params=_cparams(("arbitrary", "arbitrary")),
        name="in_proj_seq",
    )(x, mods, mods, g1.reshape(1, d), wu, wqt, wkvt, wkn, wgt, wbr)


def _ln_silu(y, g, b):
    mu = jnp.mean(y, axis=-1, keepdims=True)
    yc = y - mu
    var = jnp.mean(yc * yc, axis=-1, keepdims=True)
    return _silu(yc * lax.rsqrt(var + EPS) * g + b)


def _conv_seq_kernel(u_ref, halo_ref, w_ref, b_ref, lg_ref, lb_ref, o_ref, xs_ref, acc_ref, *, ts, rb):
    i = pl.program_id(1)
    xs_ref[0:CONV_HALO] = jnp.where(i > 0, halo_ref[0], 0.0)
    xs_ref[CONV_HALO:CONV_HALO + ts] = u_ref[0]
    c = u_ref.shape[2]
    lane = 128
    first = CONV_HALO - (CONV_K - 1)

    def col_body(ci, carry):
        c0 = pl.multiple_of(ci * lane, lane)
        for r0 in range(0, ts, rb):
            acc = jnp.zeros((rb, lane), F32)
            for k in range(CONV_K):
                acc = acc + w_ref[k:k + 1, pl.ds(c0, lane)] * xs_ref[pl.ds(r0 + first + k, rb), pl.ds(c0, lane)]
            acc_ref[pl.ds(r0, rb), pl.ds(c0, lane)] = acc
        return carry

    lax.fori_loop(0, c // lane, col_body, 0)
    o_ref[0] = _ln_silu(acc_ref[...] + b_ref[...], lg_ref[...], lb_ref[...]).astype(o_ref.dtype)


def _conv_seq(u, w, b, lg, lb, *, ts):
    bsz, s, c = u.shape
    hb = ts // CONV_HALO
    vec = lambda: pl.BlockSpec((1, c), lambda bb, ss: (0, 0))
    return pl.pallas_call(
        functools.partial(_conv_seq_kernel, ts=ts, rb=32),
        grid=(bsz, s // ts),
        in_specs=[pl.BlockSpec((1, ts, c), lambda bb, ss: (bb, ss, 0)),
                  pl.BlockSpec((1, CONV_HALO, c), lambda bb, ss: (bb, jnp.maximum(ss * hb - 1, 0), 0)),
                  pl.BlockSpec((CONV_K, c), lambda bb, ss: (0, 0)), vec(), vec(), vec()],
        out_specs=pl.BlockSpec((1, ts, c), lambda bb, ss: (bb, ss, 0)),
        out_shape=jax.ShapeDtypeStruct((bsz, s, c), BF16),
        scratch_shapes=[pltpu.VMEM((CONV_HALO + ts, c), F32), pltpu.VMEM((ts, c), F32)],
        compiler_params=_cparams(("arbitrary", "arbitrary")),
        name="conv_seq",
    )(u, u, w, b.reshape(1, c), lg.reshape(1, c), lb.reshape(1, c))


def _conv_step_kernel(hist_ref, u_ref, w_ref, b_ref, lg_ref, lb_ref, o_ref):
    hist = hist_ref[:, 0]
    y = jnp.sum(hist * w_ref[0:CONV_K - 1][None], axis=1)
    y = y + u_ref[...] * w_ref[CONV_K - 1:CONV_K] + b_ref[...]
    o_ref[...] = _ln_silu(y, lg_ref[...], lb_ref[...]).astype(o_ref.dtype)


def _conv_step(hist, u, w, b, lg, lb):
    db, c = u.shape
    gb = 16 if db % 16 == 0 else db
    vec = lambda: pl.BlockSpec((1, c), lambda i: (0, 0))
    return pl.pallas_call(
        _conv_step_kernel,
        grid=(db // gb,),
        in_specs=[pl.BlockSpec((gb, 1, CONV_K - 1, c), lambda i: (i, 0, 0, 0)),
                  pl.BlockSpec((gb, c), lambda i: (i, 0)),
                  pl.BlockSpec((CONV_K, c), lambda i: (0, 0)), vec(), vec(), vec()],
        out_specs=pl.BlockSpec((gb, c), lambda i: (i, 0)),
        out_shape=jax.ShapeDtypeStruct((db, c), BF16),
        compiler_params=_cparams(("arbitrary",)),
        name="conv_step",
    )(hist, u, w, b.reshape(1, c), lg.reshape(1, c), lb.reshape(1, c))


def _merge_kernel(ca_ref, no_ref, br_ref, x_ref, m2_ref, sh_ref, sc_ref, g2_ref,
                  wco_ref, wno_ref, wo_ref, x1_ref, h2_ref, *, nsa_t):
    d = x_ref.shape[2]
    ya = _dot(ca_ref[0], wco_ref[...])
    if nsa_t:
        yb = lax.dot_general(no_ref[0], wno_ref[...], (((0,), (0,)), ((), ())), preferred_element_type=F32)
    else:
        yb = _dot(no_ref[0], wno_ref[...])
    br = br_ref[0]
    mix = jax.nn.sigmoid(br[:, :d]) * ya + jax.nn.sigmoid(br[:, d:]) * yb
    z = _dot(mix.astype(BF16), wo_ref[...])
    x1 = x_ref[0] + m2_ref[0, 0] * z
    x1_ref[0] = x1
    r = lax.rsqrt(jnp.mean(x1 * x1, axis=-1, keepdims=True) + EPS)
    h = (x1 * r) * g2_ref[...]
    h2_ref[0] = (h * (1.0 + sc_ref[0, 0]) + sh_ref[0, 0]).astype(BF16)


def _merge(cact, nsa_o, g_br, x, mods, g2, wco, wno, wo, *, ts, nsa_t):
    b, s, d = x.shape
    row = lambda n: pl.BlockSpec((1, ts, n), lambda bb, ss: (bb, ss, 0))
    nsa_spec = pl.BlockSpec((1, nsa_o.shape[1], ts), lambda bb, ss: (bb, 0, ss)) if nsa_t else row(nsa_o.shape[2])
    return pl.pallas_call(
        functools.partial(_merge_kernel, nsa_t=nsa_t),
        grid=(b, s // ts),
        in_specs=[row(cact.shape[2]), nsa_spec, row(2 * d), row(d),
                  _mod_spec(mods, 2, ts), _mod_spec(mods, 3, ts), _mod_spec(mods, 4, ts),
                  _resident((1, d)), _resident(wco.shape), _resident(wno.shape), _resident(wo.shape)],
        out_specs=[row(d), row(d)],
        out_shape=[jax.ShapeDtypeStruct((b, s, d), F32), jax.ShapeDtypeStruct((b, s, d), BF16)],
        compiler_params=_cparams(("arbitrary", "arbitrary")),
        name="merge",
    )(cact, nsa_o, g_br, x, mods, mods, mods, g2.reshape(1, d), wco, wno, wo)


def _gelu_tanh(x):
    return 0.5 * x * (1.0 + jnp.tanh(math.sqrt(2.0 / math.pi) * (x + 0.044715 * (x * x * x))))


def _ffn_kernel(*refs, seq_mode, ts):
    if seq_mode:
        (h2_ref, x1_ref, m5_ref, gf_ref, wa_ref, wb_ref, cwa_ref, cwb_ref, cba_ref, cbb_ref, wd_ref,
         y_ref, za_ref, zb_ref, acc_ref, zs_ref, carry_ref) = refs
    else:
        (h2_ref, x1_ref, m5_ref, gf_ref, wa_ref, wb_ref, cwa_ref, cwb_ref, cba_ref, cbb_ref, wd_ref,
         h0a_ref, h0b_ref, h1a_ref, h1b_ref, y_ref, za_ref, zb_ref, acc_ref) = refs
    si = pl.program_id(1)
    f = pl.program_id(2)
    nf = pl.num_programs(2)
    h2 = h2_ref[0]
    halves = []
    for part, (w_ref, cw_ref, cb_ref, zt_ref) in enumerate(((wa_ref, cwa_ref, cba_ref, za_ref),
                                                           (wb_ref, cwb_ref, cbb_ref, zb_ref))):
        z = _dot(h2, w_ref[...])
        if seq_mode:
            slot = f * 2 + part
            prev = jnp.where(si > 0, carry_ref[slot], 0.0)
            zs_ref[0:8] = prev
            zs_ref[8:8 + ts] = z
            carry_ref[slot] = z[ts - 8:ts]
            zt_ref[0] = z[ts - 8:ts]
            zc = (cw_ref[0:1] * zs_ref[pl.ds(6, ts)] + cw_ref[1:2] * zs_ref[pl.ds(7, ts)]
                  + cw_ref[2:3] * z + cb_ref[...])
        else:
            h0_ref, h1_ref = ((h0a_ref, h1a_ref), (h0b_ref, h1b_ref))[part]
            zt_ref[0] = z
            zc = cw_ref[0:1] * h0_ref[...] + cw_ref[1:2] * h1_ref[...] + cw_ref[2:3] * z + cb_ref[...]
        halves.append(zc)
    act = (_gelu_tanh(halves[0]) * halves[1]).astype(BF16)
    contrib = _dot(act, wd_ref[...])

    @pl.when(f == 0)
    def _():
        acc_ref[...] = contrib

    @pl.when(f > 0)
    def _():
        acc_ref[...] += contrib

    @pl.when(f == nf - 1)
    def _():
        x2 = x1_ref[0] + m5_ref[0, 0] * acc_ref[...]
        r = lax.rsqrt(jnp.mean(x2 * x2, axis=-1, keepdims=True) + EPS)
        y_ref[0] = (x2 * r) * gf_ref[...]


def _ffn(h2, x1, mods, g_final, w_up, cw, cb, w_down, hist, *, ts, fc):
    b, s, d = x1.shape
    d_ff = w_down.shape[0]
    nf = d_ff // fc
    seq_mode = hist is None
    row = lambda n: pl.BlockSpec((1, ts, n), lambda bb, ss, ff: (bb, ss, 0))
    ca = lambda rows: pl.BlockSpec((rows, fc), lambda bb, ss, ff: (0, ff))
    cb_ = lambda rows: pl.BlockSpec((rows, fc), lambda bb, ss, ff: (0, nf + ff))
    in_specs = [row(d), row(d), _mod_spec(mods, 5, ts), pl.BlockSpec((1, d), lambda bb, ss, ff: (0, 0)),
                ca(d), cb_(d), ca(FFN_K), cb_(FFN_K), ca(1), cb_(1),
                pl.BlockSpec((fc, d), lambda bb, ss, ff: (ff, 0))]
    args = [h2, x1, mods, g_final.reshape(1, d), w_up, w_up, cw, cw, cb.reshape(1, -1), cb.reshape(1, -1), w_down]
    zrows = 8 * (s // ts) if seq_mode else ts
    scratch = [pltpu.VMEM((ts, d), F32)]
    if seq_mode:
        scratch += [pltpu.VMEM((ts + 8, fc), F32), pltpu.VMEM((2 * nf, 8, fc), F32)]
        zspec = pl.BlockSpec((1, 8, fc), lambda bb, ss, ff: (bb, ss, ff))
    else:
        assert b == 1 and s == ts
        h0, h1 = hist
        in_specs += [ca(ts), cb_(ts), ca(ts), cb_(ts)]
        args += [h0, h0, h1, h1]
        zspec = pl.BlockSpec((1, ts, fc), lambda bb, ss, ff: (bb, 0, ff))
    return pl.pallas_call(
        functools.partial(_ffn_kernel, seq_mode=seq_mode, ts=ts),
        grid=(b, s // ts, nf),
        in_specs=in_specs,
        out_specs=[row(d), zspec, zspec],
        out_shape=[jax.ShapeDtypeStruct((b, s, d), F32),
                   jax.ShapeDtypeStruct((b, zrows, d_ff), F32),
                   jax.ShapeDtypeStruct((b, zrows, d_ff), F32)],
        scratch_shapes=scratch,
        compiler_params=_cparams(("arbitrary", "arbitrary", "arbitrary")),
        name="ffn",
    )(*args)


def _t5_bucket(d):
    max_exact = N_BUCKETS // 2
    df = jnp.maximum(d, 1).astype(F32)
    large = max_exact + (jnp.log(df / max_exact) / math.log(MAX_DISTANCE / max_exact)
                         * (N_BUCKETS - max_exact)).astype(jnp.int32)
    large = jnp.minimum(large, N_BUCKETS - 1)
    return jnp.where(d < max_exact, d, large)


def _bias_kernel(rb_ref, o_ref, *, off, ostride, rs, cs, dmax):
    hh = pl.program_id(0)
    o = pl.program_id(1)
    rows, cols = o_ref.shape[2], o_ref.shape[3]
    d = (off + o * ostride + rs * lax.broadcasted_iota(jnp.int32, (rows, cols), 0)
         + cs * lax.broadcasted_iota(jnp.int32, (rows, cols), 1))
    bucket = _t5_bucket(jnp.maximum(d, 0))
    val = jnp.zeros((rows, cols), F32)
    for bk in range(N_BUCKETS):
        val = jnp.where(bucket == bk, rb_ref[bk, hh], val)
    ok = d >= 0
    if dmax is not None:
        ok = ok & (d <= dmax)
    o_ref[0, 0] = jnp.where(ok, val, NEG)


def _bias_table(rel_bias, n_o, rows, cols, *, off, ostride=0, rs, cs, dmax=None, grouped=False):
    if grouped:
        out_spec = pl.BlockSpec((1, 1, rows, cols), lambda hh, o: (hh // GROUP, o, 0, hh % GROUP))
        out_shape = jax.ShapeDtypeStruct((N_KV_HEADS, n_o, rows, GROUP * cols), F32)
    else:
        out_spec = pl.BlockSpec((1, 1, rows, cols), lambda hh, o: (hh, o, 0, 0))
        out_shape = jax.ShapeDtypeStruct((N_HEADS, n_o, rows, cols), F32)
    return pl.pallas_call(
        functools.partial(_bias_kernel, off=off, ostride=ostride, rs=rs, cs=cs, dmax=dmax),
        grid=(N_HEADS, n_o),
        in_specs=[pl.BlockSpec(memory_space=pltpu.SMEM)],
        out_specs=out_spec,
        out_shape=out_shape,
        compiler_params=_cparams(("arbitrary", "arbitrary")),
        name="bias_table",
    )(rel_bias)


def _compress_rows(x, pe, w1b, w2b):
    r = x.shape[0]
    xb = (x.reshape(r // CMP_BLK, CMP_BLK, HEAD_DIM) + pe[None]).reshape(r, HEAD_DIM).astype(BF16)
    h = _silu(_dot(xb, w1b))
    hm = jnp.sum(h.reshape(r // CMP_BLK, CMP_BLK, HEAD_DIM), axis=1) * (1.0 / CMP_BLK)
    return _dot(hm.astype(BF16), w2b)


def _topk_axis0(score, n_top):
    ns = score.shape[0]
    js = lax.broadcasted_iota(jnp.int32, score.shape, 0)
    sel = jnp.zeros(score.shape, jnp.bool_)
    winners = []
    for _ in range(n_top):
        m = jnp.max(score, axis=0, keepdims=True)
        first = jnp.min(jnp.where(score == m, js, ns), axis=0, keepdims=True)
        hit = js == first
        sel = sel | hit
        score = jnp.where(hit, REMOVED, score)
        winners.append(first)
    return sel, winners


def _nsa_step_cmp_kernel(q_ref, past_ref, new_ref, pe_ref, w1_ref, w2_ref, bcp_ref, bcn_ref,
                         oc_ref, idx_ref, kn_ref, imp_ref, impt_ref, *, gb, past_len, n_top):
    hd = HEAD_DIM
    ncp = past_ref.shape[4]
    nnew = bcn_ref.shape[1]
    ratio = SEL_BLK // CMP_BLK
    new_pad = SEL_BLK
    rows = gb * N_KV_HEADS * new_pad
    cm_new = []
    for slot in range(2):
        r_in_blk = lax.broadcasted_iota(jnp.int32, (gb * N_KV_HEADS, new_pad, hd), 1)
        xnew = new_ref[:, slot * N_KV_HEADS:(slot + 1) * N_KV_HEADS, :].reshape(gb * N_KV_HEADS, 1, hd)
        x = jnp.where(r_in_blk == 0, xnew, 0.0).reshape(rows, hd)
        cm_new.append(_compress_rows(x, pe_ref[slot], w1_ref[slot].astype(BF16), w2_ref[slot].astype(BF16)))
    nb_new = new_pad // CMP_BLK
    kn_ref[...] = jnp.zeros(kn_ref.shape, F32)
    imp_ref[...] = jnp.zeros(imp_ref.shape, F32)
    for bi in range(gb):
        for k in range(N_KV_HEADS):
            r0 = (bi * N_KV_HEADS + k) * nb_new
            kn_ref[0, 0:nb_new] = cm_new[0][r0:r0 + nb_new]
            kn_ref[1, 0:nb_new] = cm_new[1][r0:r0 + nb_new]
            qg = q_ref[bi, k * GROUP:(k + 1) * GROUP, :]
            bias = jnp.concatenate([bcp_ref[k * GROUP:(k + 1) * GROUP, :],
                                    bcn_ref[k * GROUP:(k + 1) * GROUP, :]], axis=1)
            lc = jnp.concatenate([_dot(qg, past_ref[bi, 0, k].astype(BF16)),
                                  _dot_nt(qg, kn_ref[0].astype(BF16))], axis=1) + bias
            mask = bias > 0.5 * NEG
            m = jnp.max(lc, axis=-1, keepdims=True)
            e = jnp.where(mask, jnp.exp(lc - m), 0.0)
            p = e / jnp.maximum(jnp.sum(e, axis=-1, keepdims=True), 1e-30)
            pb = p.astype(BF16)
            oc_ref[bi, k * GROUP:(k + 1) * GROUP, :] = (_dot_nt(pb[:, :ncp], past_ref[bi, 1, k].astype(BF16))
                                                        + _dot(pb[:, ncp:], kn_ref[1].astype(BF16)))
            row = bi * N_KV_HEADS + k
            imp_ref[row:row + 1, :] = jnp.sum(p, axis=0, keepdims=True)
    impt_ref[...] = imp_ref[...].T
    nsr = (ncp + nnew) // ratio
    imps = impt_ref[pl.ds(0, nsr, stride=ratio), :]
    for rr in range(1, ratio):
        imps = imps + impt_ref[pl.ds(rr, nsr, stride=ratio), :]
    ns = (past_len + new_pad) // SEL_BLK
    js = lax.broadcasted_iota(jnp.int32, imps.shape, 0)
    qblk = past_len // SEL_BLK
    forced = (js == 0) | (js == qblk) | (js == qblk - 1)
    score = jnp.where(forced, FORCE_SCORE, jnp.where(js * SEL_BLK <= past_len, imps, -1.0))
    score = jnp.where(js < ns, score, REMOVED)
    _, winners = _topk_axis0(score, n_top)
    idx_ref[0] = jnp.concatenate(winners, axis=0)


def _nsa_step_cmp(q, cmp_past, kv_new01, pe, w1, w2, bias_cp, bias_cn, *, gb, past_len):
    db = q.shape[0]
    ncp = cmp_past.shape[4]
    nnew = bias_cn.shape[1]
    hd = HEAD_DIM
    ns = (past_len + SEL_BLK) // SEL_BLK
    n_top = min(N_SEL, ns)
    assert gb * N_KV_HEADS <= LANE
    full = lambda a: pl.BlockSpec(a.shape, lambda i: (0,) * a.ndim)
    return pl.pallas_call(
        functools.partial(_nsa_step_cmp_kernel, gb=gb, past_len=past_len, n_top=n_top),
        grid=(db // gb,),
        in_specs=[pl.BlockSpec((gb, N_HEADS, hd), lambda i: (i, 0, 0)),
                  pl.BlockSpec((gb, 2, N_KV_HEADS, hd, ncp), lambda i: (i, 0, 0, 0, 0)),
                  pl.BlockSpec((gb, 2 * N_KV_HEADS, hd), lambda i: (i, 0, 0)),
                  full(pe), full(w1), full(w2), full(bias_cp), full(bias_cn)],
        out_specs=[pl.BlockSpec((gb, N_HEADS, hd), lambda i: (i, 0, 0)),
                   pl.BlockSpec((1, n_top, LANE), lambda i: (i, 0, 0))],
        out_shape=[jax.ShapeDtypeStruct((db, N_HEADS, hd), F32),
                   jax.ShapeDtypeStruct((db // gb, n_top, LANE), jnp.int32)],
        scratch_shapes=[pltpu.VMEM((2, nnew, hd), F32), pltpu.VMEM((LANE, ncp + nnew), F32),
                        pltpu.VMEM((ncp + nnew, LANE), F32)],
        compiler_params=_cparams(("arbitrary",)),
        name="nsa_step_cmp",
    )(q, cmp_past, kv_new01, pe, w1, w2, bias_cp, bias_cn)


def _compress_t(xt, pet, w1t, w2t, pool):
    xb = (xt + pet).astype(BF16)
    h = _silu(_dot(w1t, xb))
    hi = h.astype(BF16)
    lo = (h - hi.astype(F32)).astype(BF16)
    hm = _dot(hi, pool) + _dot(lo, pool)
    return _dot(w2t, hm.astype(BF16))


def _pool_matrix(r):
    rows = lax.broadcasted_iota(jnp.int32, (r, r // CMP_BLK), 0) // CMP_BLK
    cols = lax.broadcasted_iota(jnp.int32, (r, r // CMP_BLK), 1)
    return jnp.where(rows == cols, 1.0 / CMP_BLK, 0.0).astype(BF16)


def _compress_seq_t_kernel(x_ref, pe_ref, w1_ref, w2_ref, pool_ref, on_ref, ot_ref):
    ct = _compress_t(x_ref[0], pe_ref[0], w1_ref[0], w2_ref[0], pool_ref[...])
    ot_ref[0, 0] = ct
    on_ref[0, 0] = ct.T


def _compress_seq_t(kv4t, pet, w1t, w2t, pool):
    b, _, s = kv4t.shape
    hd = HEAD_DIM
    n = 2 * N_KV_HEADS
    nc = s // CMP_BLK
    wspec = lambda: pl.BlockSpec((1, hd, hd), lambda bb, j: (j // N_KV_HEADS, 0, 0))
    return pl.pallas_call(
        _compress_seq_t_kernel,
        grid=(b, n),
        in_specs=[pl.BlockSpec((1, hd, s), lambda bb, j: (bb, j, 0)),
                  pl.BlockSpec((1, hd, s), lambda bb, j: (j // N_KV_HEADS, 0, 0)),
                  wspec(), wspec(), pl.BlockSpec(pool.shape, lambda bb, j: (0, 0))],
        out_specs=[pl.BlockSpec((1, 1, nc, hd), lambda bb, j: (bb, j, 0, 0)),
                   pl.BlockSpec((1, 1, hd, nc), lambda bb, j: (bb, j, 0, 0))],
        out_shape=[jax.ShapeDtypeStruct((b, n, nc, hd), F32), jax.ShapeDtypeStruct((b, n, hd, nc), F32)],
        compiler_params=_cparams(("arbitrary", "arbitrary")),
        name="compress_seq",
    )(kv4t, pet, w1t, w2t, pool)


V_ROWS = 80


def _flash_step_t(k_tile, qa, v_tile, bias, shift, m_ref, acc_ref, idx):
    s = _dot(k_tile, qa)
    if bias is not None:
        s = s + bias
    m_old = m_ref[idx]
    mx = jnp.max(s, axis=0, keepdims=True)
    if shift is not None:
        mx = mx + shift
    m_new = jnp.maximum(m_old, mx)
    alpha = jnp.exp(m_old - m_new)
    ms = m_new if shift is None else m_new - shift
    p = jnp.exp(s - ms).astype(BF16)
    acc_ref[idx] = alpha * acc_ref[idx] + _dot(v_tile, p)
    m_ref[idx] = m_new


def _nsa_seq_t_kernel(qt_ref, kn_ref, vt_ref, kc_ref, vct_ref, gt_ref, bc_ref, bs_ref, bw_ref, far_ref, o_ref,
                      kaug_ref, kwaug_ref, vs_ref, vw_ref, qa_ref, impt_ref, oc_ref, m_ref, acc_ref,
                      *, t, n_near, n_top):
    kh = pl.program_id(1)
    i = pl.program_id(2)
    s_len = kn_ref.shape[1]
    ns = s_len // SEL_BLK
    hd = HEAD_DIM

    @pl.when(i == 0)
    def _():
        blk = lax.broadcasted_iota(jnp.int32, (s_len, ns), 0) // SEL_BLK
        col = lax.broadcasted_iota(jnp.int32, (s_len, ns), 1)
        kaug_ref[:, 0:hd] = kn_ref[0, :, 0:hd]
        kaug_ref[:, hd:hd + ns] = jnp.where(blk == col, NEG, 0.0).astype(BF16)
        kwaug_ref[:, 0:hd] = kn_ref[0, :, hd:2 * hd]
        kwaug_ref[:, hd:hd + ns] = jnp.zeros((s_len, ns), BF16)
        row = lax.broadcasted_iota(jnp.int32, (V_ROWS - hd, s_len), 0)
        tail = jnp.where(row == 0, 1.0, 0.0).astype(BF16)
        vs_ref[0:hd] = vt_ref[0, 0:hd]
        vs_ref[hd:V_ROWS] = tail
        vw_ref[0:hd] = vt_ref[0, hd:2 * hd]
        vw_ref[hd:V_ROWS] = tail

    for g in range(GROUP):
        qa_ref[0:hd, g * t:(g + 1) * t] = qt_ref[0, g * hd:(g + 1) * hd, :]
    kc = kc_ref[0, 0].astype(BF16)
    vct = vct_ref[0, 0].astype(BF16)
    bias = bc_ref[0, 0]
    lc = _dot(kc, qa_ref[0:hd, :]) + bias
    mask = bias > 0.5 * NEG
    m = jnp.max(lc, axis=0, keepdims=True)
    e = jnp.where(mask, jnp.exp(lc - m), 0.0)
    p = e / jnp.maximum(jnp.sum(e, axis=0, keepdims=True), 1e-30)
    oc_ref[...] = _dot(vct, p.astype(BF16))
    imp = p[:, 0:t]
    for g in range(1, GROUP):
        imp = imp + p[:, g * t:(g + 1) * t]
    ratio = SEL_BLK // CMP_BLK
    parts = []
    for c in range(t // LANE):
        impt_ref[c] = imp[:, c * LANE:(c + 1) * LANE]
        part = impt_ref[c, pl.ds(0, ns, stride=ratio), :]
        for rr in range(1, ratio):
            part = part + impt_ref[c, pl.ds(rr, ns, stride=ratio), :]
        parts.append(part)
    imps = jnp.concatenate(parts, axis=1)
    js = lax.broadcasted_iota(jnp.int32, (ns, t), 0)
    pos = i * t + lax.broadcasted_iota(jnp.int32, (ns, t), 1)
    qblk = pos // SEL_BLK
    forced = (js == 0) | (js == qblk) | (js == qblk - 1)
    score = jnp.where(forced, FORCE_SCORE, jnp.where(js * SEL_BLK <= pos, imps, -1.0))
    sel, _ = _topk_axis0(score, n_top)
    notsel = jnp.where(sel, 0.0, 1.0).astype(BF16)
    for g in range(GROUP):
        qa_ref[hd:hd + ns, g * t:(g + 1) * t] = notsel

    m_ref[...] = jnp.full(m_ref.shape, NEG, F32)
    acc_ref[...] = jnp.zeros(acc_ref.shape, F32)
    n_far = jnp.maximum(i - (n_near - 1), 0)
    far_shift = jnp.concatenate([jnp.full((1, t), far_ref[kh * GROUP + g], F32) for g in range(GROUP)], axis=1)

    def sel_step(r0, rows, bias, shift):
        _flash_step_t(kaug_ref[pl.ds(r0, rows), :], qa_ref[...], vs_ref[:, pl.ds(r0, rows)],
                      bias, shift, m_ref, acc_ref, 0)

    def far_body(jj, carry):
        sel_step(pl.multiple_of(jj * 2 * t, 2 * t), 2 * t, None, far_shift)
        return carry

    lax.fori_loop(0, n_far // 2, far_body, 0)

    @pl.when(n_far % 2 == 1)
    def _():
        sel_step(pl.multiple_of((n_far - 1) * t, t), t, None, far_shift)

    def tiled(step, bias_ref, n):
        def body(j, carry):
            r = pl.multiple_of((n - 1 - (i - j)) * t, t)
            step(pl.multiple_of(j * t, t), t, bias_ref[0, pl.ds(r, t), :])
            return carry
        lax.fori_loop(0, i + 1, body, 0)

    @pl.when(i >= n_near - 1)
    def _():
        sel_step(pl.multiple_of((i - (n_near - 1)) * t, t), n_near * t, bs_ref[0], None)

    @pl.when(i < n_near - 1)
    def _():
        tiled(lambda r0, rows, bias: sel_step(r0, rows, bias, None), bs_ref, n_near)

    n_wt = WINDOW // t + 1

    def win_step(r0, rows, bias):
        _flash_step_t(kwaug_ref[pl.ds(r0, rows), :], qa_ref[...], vw_ref[:, pl.ds(r0, rows)],
                      bias, None, m_ref, acc_ref, 1)

    @pl.when(i >= n_wt - 1)
    def _():
        win_step(pl.multiple_of((i - (n_wt - 1)) * t, t), n_wt * t, bw_ref[0])

    @pl.when(i < n_wt - 1)
    def _():
        tiled(win_step, bw_ref, n_wt)

    def gate(r):
        rows = [gt_ref[0, pl.ds((kh * GROUP + g) * 3 + r, 1), :] for g in range(GROUP)]
        return jax.nn.sigmoid(jnp.concatenate(rows, axis=1))

    acc_s = acc_ref[0]
    acc_w = acc_ref[1]
    o_s = acc_s[0:hd] / acc_s[hd:hd + 1]
    o_w = acc_w[0:hd] / acc_w[hd:hd + 1]
    o = gate(0) * oc_ref[...] + gate(1) * o_s + gate(2) * o_w
    for g in range(GROUP):
        o_ref[0, g * hd:(g + 1) * hd, :] = o[:, g * t:(g + 1) * t].astype(o_ref.dtype)


def _nsa_seq_t(qt, kn, vt, cmp_n, cmp_t, gt, rel_bias, *, t):
    b, nq, s = qt.shape
    hd = HEAD_DIM
    ns = s // SEL_BLK
    nc = s // CMP_BLK
    n_top = min(N_SEL, ns)
    assert ns == hd and s % t == 0 and t % LANE == 0 and WINDOW % t == 0
    n_near = min(-(-(MAX_DISTANCE + t - 1) // t), s // t)
    n_wt = WINDOW // t + 1
    assert n_wt <= n_near + 1
    bias_s = _bias_table(rel_bias, n_near, t, t, off=(n_near - 1) * t, ostride=-t, rs=-1, cs=1, grouped=True)
    bias_s = bias_s.reshape(N_KV_HEADS, n_near * t, GROUP * t)
    bias_w = _bias_table(rel_bias, n_wt, t, t, off=(n_wt - 1) * t, ostride=-t, rs=-1, cs=1, dmax=WINDOW,
                         grouped=True)
    bias_w = bias_w.reshape(N_KV_HEADS, n_wt * t, GROUP * t)
    bias_c = _bias_table(rel_bias, s // t, nc, t, off=-(CMP_BLK - 1), ostride=t, rs=-CMP_BLK, cs=1, grouped=True)
    far = rel_bias[N_BUCKETS - 1]
    gw = GROUP * hd
    gt_ = GROUP * t
    return pl.pallas_call(
        functools.partial(_nsa_seq_t_kernel, t=t, n_near=n_near, n_top=n_top),
        grid=(b, N_KV_HEADS, s // t),
        in_specs=[pl.BlockSpec((1, gw, t), lambda bb, k, i: (bb, k, i)),
                  pl.BlockSpec((1, s, 2 * hd), lambda bb, k, i: (bb, 0, k)),
                  pl.BlockSpec((1, 2 * hd, s), lambda bb, k, i: (bb, k, 0)),
                  pl.BlockSpec((1, 1, nc, hd), lambda bb, k, i: (bb, k, 0, 0)),
                  pl.BlockSpec((1, 1, hd, nc), lambda bb, k, i: (bb, N_KV_HEADS + k, 0, 0)),
                  pl.BlockSpec((1, gt.shape[1], t), lambda bb, k, i: (bb, 0, i)),
                  pl.BlockSpec((1, 1, nc, gt_), lambda bb, k, i: (k, i, 0, 0)),
                  pl.BlockSpec((1, n_near * t, gt_), lambda bb, k, i: (k, 0, 0)),
                  pl.BlockSpec((1, n_wt * t, gt_), lambda bb, k, i: (k, 0, 0)),
                  pl.BlockSpec(memory_space=pltpu.SMEM)],
        out_specs=pl.BlockSpec((1, gw, t), lambda bb, k, i: (bb, k, i)),
        out_shape=jax.ShapeDtypeStruct((b, nq, s), BF16),
        scratch_shapes=[pltpu.VMEM((s, 2 * hd), BF16), pltpu.VMEM((s, 2 * hd), BF16),
                        pltpu.VMEM((V_ROWS, s), BF16), pltpu.VMEM((V_ROWS, s), BF16),
                        pltpu.VMEM((2 * hd, gt_), BF16), pltpu.VMEM((t // LANE, nc, LANE), F32),
                        pltpu.VMEM((hd, gt_), F32),
                        pltpu.VMEM((2, 1, gt_), F32), pltpu.VMEM((2, V_ROWS, gt_), F32)],
        compiler_params=_cparams(("arbitrary", "arbitrary", "arbitrary")),
        name="nsa_seq",
    )(qt, kn, vt, cmp_n, cmp_t, gt, bias_c, bias_s, bias_w, far)


def _compress_pages_t_kernel(pt_ref, *refs, pg):
    page_refs = refs[:pg]
    pe_ref, w1_ref, w2_ref, pool_ref, o_ref = refs[pg:]
    for slot in range(2):
        for hh in range(N_KV_HEADS):
            xt = jnp.concatenate([page_refs[n][0, 0, slot, hh] for n in range(pg)], axis=1)
            o_ref[0, slot, hh] = _compress_t(xt, pe_ref[slot], w1_ref[slot], w2_ref[slot], pool_ref[...])


def _compress_pages_t(cache_t, page_table, layer, pet, w1t, w2t, pool, *, pg):
    db, n_pages = page_table.shape
    bpp = PAGE_SIZE // CMP_BLK
    hd = HEAD_DIM

    def page_map(b, p, pt, n):
        return (pt[b * n_pages + p * pg + n], layer, 0, 0, 0, 0)

    full = lambda a: pl.BlockSpec(a.shape, lambda b, p, pt: (0,) * a.ndim)
    grid_spec = pltpu.PrefetchScalarGridSpec(
        num_scalar_prefetch=1,
        grid=(db, n_pages // pg),
        in_specs=[pl.BlockSpec((1, 1, 2, N_KV_HEADS, hd, PAGE_SIZE), functools.partial(page_map, n=n))
                  for n in range(pg)] + [full(pet), full(w1t), full(w2t), full(pool)],
        out_specs=pl.BlockSpec((1, 2, N_KV_HEADS, hd, pg * bpp), lambda b, p, pt: (b, 0, 0, 0, p)),
    )
    return pl.pallas_call(
        functools.partial(_compress_pages_t_kernel, pg=pg),
        grid_spec=grid_spec,
        out_shape=jax.ShapeDtypeStruct((db, 2, N_KV_HEADS, hd, n_pages * bpp), F32),
        compiler_params=_cparams(("arbitrary", "arbitrary")),
        name="compress_pages",
    )(page_table.reshape(-1), *([cache_t] * pg), pet, w1t, w2t, pool)


def _nsa_step_sel_t_kernel(idx_ref, pt_ref, *refs, n_top, nsp):
    blk_refs = refs[:n_top]
    (q_ref, win_ref, new_ref, newt_ref, oc_ref, g_ref, bsel_ref, bwin_ref, o_ref, ks_ref, vs_ref, bs_ref) = refs[n_top:]
    b = pl.program_id(0)
    k = pl.program_id(1)
    hd = HEAD_DIM
    bpp = PAGE_SIZE // SEL_BLK
    qg = q_ref[0, 0]
    new = new_ref[0, 0]
    newt = newt_ref[0, 0]
    lane = lax.broadcasted_iota(jnp.int32, (hd, PAGE_SIZE), 1)
    k_newblk = jnp.where(lane == 0, newt[:, 2:3], 0.0)
    v_newblk = jnp.where(lane == 0, newt[:, 3:4], 0.0)
    for n in range(n_top):
        idn = idx_ref[(b * N_KV_HEADS + k) * n_top + n]
        is_new = idn >= nsp
        half = jnp.where(is_new, 0, jnp.minimum(idn, nsp - 1) % bpp)
        ks_ref[:, n * PAGE_SIZE:(n + 1) * PAGE_SIZE] = jnp.where(is_new, k_newblk, blk_refs[n][0, 0, 0, 0]).astype(BF16)
        vs_ref[:, n * PAGE_SIZE:(n + 1) * PAGE_SIZE] = jnp.where(is_new, v_newblk, blk_refs[n][0, 0, 1, 0]).astype(BF16)
        for g in range(GROUP):
            brow = bsel_ref[g, 0, pl.ds(idn, 1), :]
            for hf in range(bpp):
                c0 = n * PAGE_SIZE + hf * SEL_BLK
                bs_ref[g:g + 1, c0:c0 + SEL_BLK] = jnp.where(half == hf, brow, NEG)
    bias = bs_ref[...]
    ls = _dot(qg, ks_ref[...]) + bias
    mask = bias > 0.5 * NEG
    m = jnp.max(ls, axis=-1, keepdims=True)
    e = jnp.where(mask, jnp.exp(ls - m), 0.0)
    p = e / jnp.maximum(jnp.sum(e, axis=-1, keepdims=True), 1e-30)
    o_s = _dot_nt(p.astype(BF16), vs_ref[...])
    wb = win_ref.shape[5]
    bw = bwin_ref[0]
    bias_p = bw[:, 0:wb]
    bias_n = bw[:, wb:wb + 1]
    kn = new[4:5].astype(BF16).astype(F32)
    vn = new[5:6].astype(BF16).astype(F32)
    lw = _dot(qg, win_ref[0, 0, 0, 0].astype(BF16)) + bias_p
    lwn = jnp.sum(qg.astype(F32) * kn, axis=-1, keepdims=True) + bias_n
    mask_p = bias_p > 0.5 * NEG
    mask_n = bias_n > 0.5 * NEG
    m = jnp.maximum(jnp.max(lw, axis=-1, keepdims=True), lwn)
    e_p = jnp.where(mask_p, jnp.exp(lw - m), 0.0)
    e_n = jnp.where(mask_n, jnp.exp(lwn - m), 0.0)
    den = jnp.maximum(jnp.sum(e_p, axis=-1, keepdims=True) + e_n, 1e-30)
    o_w = (_dot_nt((e_p / den).astype(BF16), win_ref[0, 0, 1, 0].astype(BF16))
           + (e_n / den).astype(BF16).astype(F32) * vn)
    gates = jnp.broadcast_to(jax.nn.sigmoid(g_ref[0]), (GROUP, g_ref.shape[2]))
    col = lax.broadcasted_iota(jnp.int32, gates.shape, 1)
    head = k * GROUP + lax.broadcasted_iota(jnp.int32, gates.shape, 0)
    gate = lambda r: jnp.sum(jnp.where(col == head * 3 + r, gates, 0.0), axis=-1, keepdims=True)
    o_ref[0, 0] = gate(0) * oc_ref[0, 0] + gate(1) * o_s + gate(2) * o_w


def _nsa_step_sel_t(idx, page_table, cache_t, layer, q, win_t, kv_new, kv_new_t, o_c, g_nsa, bias_sel, bias_win,
                    *, past_len):
    db, n_pages = page_table.shape
    n_top = idx.shape[-1]
    hd = HEAD_DIM
    nsp = past_len // SEL_BLK
    bpp = PAGE_SIZE // SEL_BLK
    wb = win_t.shape[5]

    def blk_map(b, k, ix, pt, n):
        jp = jnp.minimum(ix[(b * N_KV_HEADS + k) * n_top + n], nsp - 1)
        return (pt[b * n_pages + jp // bpp], layer, 1, k, 0, 0)

    grid_spec = pltpu.PrefetchScalarGridSpec(
        num_scalar_prefetch=2,
        grid=(db, N_KV_HEADS),
        in_specs=[pl.BlockSpec((1, 1, 2, 1, hd, PAGE_SIZE), functools.partial(blk_map, n=n)) for n in range(n_top)]
        + [pl.BlockSpec((1, 1, GROUP, hd), lambda b, k, ix, pt: (b, k, 0, 0)),
           pl.BlockSpec((1, 1, 2, 1, hd, wb), lambda b, k, ix, pt: (b, layer, 0, k, 0, 0)),
           pl.BlockSpec((1, 1, 6, hd), lambda b, k, ix, pt: (b, k, 0, 0)),
           pl.BlockSpec((1, 1, hd, 8), lambda b, k, ix, pt: (b, k, 0, 0)),
           pl.BlockSpec((1, 1, GROUP, hd), lambda b, k, ix, pt: (b, k, 0, 0)),
           pl.BlockSpec((1, 1, g_nsa.shape[-1]), lambda b, k, ix, pt: (b, 0, 0)),
           pl.BlockSpec((GROUP, 1) + bias_sel.shape[2:], lambda b, k, ix, pt: (k, 0, 0, 0)),
           pl.BlockSpec((1, GROUP, bias_win.shape[-1]), lambda b, k, ix, pt: (k, 0, 0))],
        out_specs=pl.BlockSpec((1, 1, GROUP, hd), lambda b, k, ix, pt: (b, k, 0, 0)),
        scratch_shapes=[pltpu.VMEM((hd, n_top * PAGE_SIZE), BF16), pltpu.VMEM((hd, n_top * PAGE_SIZE), BF16),
                        pltpu.VMEM((GROUP, n_top * PAGE_SIZE), F32)],
    )
    return pl.pallas_call(
        functools.partial(_nsa_step_sel_t_kernel, n_top=n_top, nsp=nsp),
        grid_spec=grid_spec,
        out_shape=jax.ShapeDtypeStruct((db, N_KV_HEADS, GROUP, hd), F32),
        compiler_params=_cparams(("arbitrary", "arbitrary")),
        name="nsa_step_sel",
    )(idx.reshape(-1), page_table.reshape(-1), *([cache_t] * n_top), q, win_t, kv_new, kv_new_t, o_c, g_nsa,
      bias_sel, bias_win)


def _pick(n, pref):
    for c in pref:
        if n % c == 0:
            return c
    return n


def kernel(x_prompt, x_sample, cache_kv, state_win_kv, state_conv, state_ffn, page_table, c_prompt, c_sample,
           w_ada, b_ada, g_norm1, g_norm2, w_in, conv_w, conv_b, conv_ln_g, conv_ln_b, w_conv_out,
           cmp_pe, cmp_w1, cmp_w2, w_nsa_out, w_out, w_up, ffn_conv_w, ffn_conv_b, w_down, rel_bias, g_final):
    depth = w_ada.shape[0]
    assert depth == 1
    layer = 0
    b, s, d = x_prompt.shape
    db = x_sample.shape[0]
    assert x_sample.shape[1] == 1
    hd = HEAD_DIM
    d_conv = conv_w.shape[2]
    d_ff = w_down.shape[1]
    nq = N_HEADS * hd
    nkv = 6 * N_KV_HEADS * hd
    past_len = page_table.shape[1] * PAGE_SIZE
    wb = state_win_kv.shape[4]

    o0, o1, o2, o3 = 2 * d_conv, 2 * d_conv + nq, 2 * d_conv + nq + nkv, 2 * d_conv + nq + nkv + 3 * N_HEADS
    wi = w_in[layer]
    wu = wi[:, :o0].astype(BF16)
    wq = wi[:, o0:o1].astype(BF16)
    wkv = wi[:, o1:o2].astype(BF16)
    wkv6 = wkv.reshape(d, 6, N_KV_HEADS, hd)
    wkn = jnp.stack([wkv6[:, 2], wkv6[:, 4]], axis=2).reshape(d, 2 * N_KV_HEADS * hd)
    wg = jnp.pad(wi[:, o2:o3], ((0, 0), (0, 128 - 3 * N_HEADS))).astype(BF16)
    wbr = wi[:, o3:].astype(BF16)
    wco = w_conv_out[layer].astype(BF16)
    wno = w_nsa_out[layer].astype(BF16)
    wo = w_out[layer].astype(BF16)
    wup = w_up[layer].astype(BF16)
    wdn = w_down[layer].astype(BF16)

    mod = _ada(jnp.concatenate([c_prompt, c_sample], axis=0), w_ada[layer], b_ada[layer])
    mod = mod.reshape(b + db, 6, d)
    mods_p = mod[:b].transpose(1, 0, 2).reshape(6, b, 1, d)
    mods_s = mod[b:].transpose(1, 0, 2).reshape(6, 1, db, d)

    ts = _pick(s, (256, 128))
    u_p, qt_p, kn_p, vt_p, kv4t_p, kvwt_p, gt_p, br_p = _proj_seq(
        x_prompt, mods_p, g_norm1[layer], wu, wq.T, wkv.T, wkn, wg.T, wbr, ts=ts)
    cact_p = _conv_seq(u_p, conv_w[layer], conv_b[layer], conv_ln_g[layer], conv_ln_b[layer], ts=ts)
    pe_t = cmp_pe[layer].transpose(0, 2, 1)
    w1t = cmp_w1[layer].transpose(0, 2, 1).astype(BF16)
    w2t = cmp_w2[layer].transpose(0, 2, 1).astype(BF16)
    cmpn_p, cmpt_p = _compress_seq_t(kv4t_p, jnp.tile(pe_t, (1, 1, s // CMP_BLK)), w1t, w2t, _pool_matrix(s))
    ot_p = _nsa_seq_t(qt_p, kn_p, vt_p, cmpn_p, cmpt_p, gt_p, rel_bias, t=256)
    x1_p, h2_p = _merge(cact_p, ot_p, br_p, x_prompt, mods_p, g_norm2[layer], wco, wno, wo,
                        ts=_pick(s, (512, 256)), nsa_t=True)
    y_prompt, zta, ztb = _ffn(h2_p, x1_p, mods_p, g_final, wup, ffn_conv_w[layer], ffn_conv_b[layer], wdn, None,
                              ts=_pick(s, (512, 256)), fc=512)
    w_len = min(WINDOW, s)
    kv_prompt = jnp.swapaxes(kv4t_p.reshape(b, 1, 4, N_KV_HEADS, hd, s), -1, -2)
    win_prompt = jnp.swapaxes(kvwt_p.reshape(b, 1, 2, N_KV_HEADS, hd, s)[..., s - w_len:], -1, -2)
    conv_prompt = u_p[:, None, s - (CONV_K - 1):]
    ffn_prompt = jnp.concatenate([zta, ztb], axis=-1)[:, None, zta.shape[1] - (FFN_K - 1):]

    xs = x_sample.reshape(1, db, d)
    u_s, q_s, kv_s, g_s, br_s = _proj_step(xs, mods_s, g_norm1[layer], wu, wq, wkv, wg, wbr, ts=db)
    kv_s = kv_s.reshape(db, 6, N_KV_HEADS, hd)
    cact_s = _conv_step(state_conv[:, layer:layer + 1], u_s[0], conv_w[layer], conv_b[layer],
                        conv_ln_g[layer], conv_ln_b[layer])
    cache_t = jnp.swapaxes(cache_kv, -1, -2)
    win_t = jnp.swapaxes(state_win_kv, -1, -2)
    pg = LANE // (PAGE_SIZE // CMP_BLK)
    assert page_table.shape[1] % pg == 0
    rp = pg * PAGE_SIZE
    cmp_past = _compress_pages_t(cache_t, page_table, layer, jnp.tile(pe_t, (1, 1, rp // CMP_BLK)), w1t, w2t,
                                 _pool_matrix(rp), pg=pg)
    ncp = cmp_past.shape[4]
    nnew = 128
    bias_cp = _bias_table(rel_bias, 1, 8, ncp, off=past_len - (CMP_BLK - 1), rs=0, cs=-CMP_BLK)[:, 0, 0]
    bias_cn = _bias_table(rel_bias, 1, 8, nnew, off=past_len - (CMP_BLK - 1) - ncp * CMP_BLK, rs=0, cs=-CMP_BLK)[:, 0, 0]
    gb = _pick(db, (8,))
    oc_s, idx_t = _nsa_step_cmp(q_s.reshape(db, N_HEADS, hd), cmp_past, kv_s[:, :2].reshape(db, 2 * N_KV_HEADS, hd),
                                cmp_pe[layer], cmp_w1[layer], cmp_w2[layer], bias_cp, bias_cn, gb=gb,
                                past_len=past_len)
    n_top = idx_t.shape[1]
    idx = idx_t[:, :, :gb * N_KV_HEADS].reshape(db // gb, n_top, gb, N_KV_HEADS).transpose(0, 2, 3, 1).reshape(db, N_KV_HEADS, n_top)
    ns_tot = (past_len + SEL_BLK) // SEL_BLK
    ns_rows = -(-ns_tot // 8) * 8
    bias_sel = _bias_table(rel_bias, 1, ns_rows, SEL_BLK, off=past_len, rs=-SEL_BLK, cs=-1)
    wcols = -(-(wb + 1) // 128) * 128
    bias_win = _bias_table(rel_bias, 1, 8, wcols, off=wb, rs=0, cs=-1, dmax=WINDOW)[:, 0, 0]
    bias_win = bias_win.reshape(N_KV_HEADS, GROUP, wcols)
    kv_new = kv_s.transpose(0, 2, 1, 3)
    kv_new_t = jnp.pad(kv_s.transpose(0, 2, 3, 1), ((0, 0), (0, 0), (0, 0), (0, 2)))
    o_s = _nsa_step_sel_t(idx, page_table, cache_t, layer, q_s.reshape(db, N_KV_HEADS, GROUP, hd), win_t,
                          kv_new, kv_new_t, oc_s.reshape(db, N_KV_HEADS, GROUP, hd),
                          g_s.reshape(db, 1, -1), bias_sel, bias_win, past_len=past_len)
    o_s = o_s.reshape(1, db, nq).astype(BF16)
    x1_s, h2_s = _merge(cact_s.reshape(1, db, d_conv), o_s, br_s, xs, mods_s, g_norm2[layer], wco, wno, wo,
                        ts=db, nsa_t=False)
    hist_f = state_ffn[:, layer]
    y_s, za_s, zb_s = _ffn(h2_s, x1_s, mods_s, g_final, wup, ffn_conv_w[layer], ffn_conv_b[layer], wdn,
                           (hist_f[:, 0], hist_f[:, 1]), ts=db, fc=512)
    y_sample = y_s.reshape(db, 1, d)
    kv_sample = kv_s[:, :4].reshape(db, 1, 4, N_KV_HEADS, 1, hd)
    win_sample = jnp.concatenate([state_win_kv[:, layer, :, :, 1:], kv_s[:, 4:, :, None, :]], axis=3)[:, None]
    conv_sample = jnp.concatenate([state_conv[:, layer, 1:], u_s[0][:, None, :]], axis=1)[:, None]
    z_s = jnp.concatenate([za_s[0], zb_s[0]], axis=-1)
    ffn_sample = jnp.concatenate([hist_f[:, 1:], z_s[:, None, :]], axis=1)[:, None]
    return (y_prompt, y_sample, kv_prompt, kv_sample, win_prompt, win_sample, conv_prompt, conv_sample,
            ffn_prompt, ffn_sample)
```

```python
import functools
import math

import jax
import jax.numpy as jnp
from jax import lax
from jax.experimental import pallas as pl
from jax.experimental.pallas import tpu as pltpu

F32 = jnp.float32
BF16 = jnp.bfloat16

N_HEADS = 16
HEAD_DIM = 64
N_KV_HEADS = 4
GROUP = N_HEADS // N_KV_HEADS
CMP_BLK = 32
SEL_BLK = 64
N_SEL = 16
WINDOW = 512
N_BUCKETS = 32
MAX_DISTANCE = 1024
CONV_K = 31
FFN_K = 3
PAGE_SIZE = 128
EPS = 1e-6
LOG2E = math.log2(math.e)
FORCE_SCORE = 1e4
NEG = -1e30
REMOVED = -3e38
LANE = 128
SUBLANE = 8
CONV_HALO = 32
VMEM_LIMIT = 56 * 1024 * 1024


def _cparams(sem):
    return pltpu.CompilerParams(dimension_semantics=sem, vmem_limit_bytes=VMEM_LIMIT)


def _dot(a, b):
    return jnp.dot(a, b, preferred_element_type=F32)


def _dot_nt(a, b):
    return lax.dot_general(a, b, (((1,), (1,)), ((), ())), preferred_element_type=F32)


def _silu(x):
    return x * jax.nn.sigmoid(x)


def _ada_kernel(c_ref, w_ref, b_ref, o_ref):
    s = _silu(c_ref[...]).astype(BF16)
    o_ref[...] = _dot(s, w_ref[...].astype(BF16)) + b_ref[...]


def _ada(c, w, b):
    m, d = c.shape
    n = w.shape[1]
    tn = 1536 if n % 1536 == 0 else n
    return pl.pallas_call(
        _ada_kernel,
        grid=(n // tn,),
        in_specs=[pl.BlockSpec((m, d), lambda j: (0, 0)),
                  pl.BlockSpec((d, tn), lambda j: (0, j)),
                  pl.BlockSpec((1, tn), lambda j: (0, j))],
        out_specs=pl.BlockSpec((m, tn), lambda j: (0, j)),
        out_shape=jax.ShapeDtypeStruct((m, n), F32),
        compiler_params=_cparams(("arbitrary",)),
        name="ada_mod",
    )(c, w, b.reshape(1, n))


def _mod_spec(mods, i, ts):
    d = mods.shape[-1]
    if mods.shape[2] == 1:
        return pl.BlockSpec((1, 1, 1, d), lambda b, s, *_: (i, b, 0, 0))
    return pl.BlockSpec((1, 1, ts, d), lambda b, s, *_: (i, b, s, 0))


def _norm_mod(x_ref, sh_ref, sc_ref, g_ref):
    x = x_ref[0]
    r = lax.rsqrt(jnp.mean(x * x, axis=-1, keepdims=True) + EPS)
    h = (x * r) * g_ref[...]
    return (h * (1.0 + sc_ref[0, 0]) + sh_ref[0, 0]).astype(BF16)


def _glu(hb, wu_ref, u_ref, d_conv):
    cw = 256
    for c in range(d_conv // cw):
        a = _dot(hb, wu_ref[:, c * cw:(c + 1) * cw])
        g = _dot(hb, wu_ref[:, d_conv + c * cw:d_conv + (c + 1) * cw])
        u_ref[0, :, c * cw:(c + 1) * cw] = a * jax.nn.sigmoid(g)


def _proj_step_kernel(x_ref, sh_ref, sc_ref, g1_ref, wu_ref, wq_ref, wkv_ref, wg_ref, wbr_ref,
                      u_ref, q_ref, kv_ref, g_ref, br_ref, *, d_conv):
    hb = _norm_mod(x_ref, sh_ref, sc_ref, g1_ref)
    _glu(hb, wu_ref, u_ref, d_conv)
    q_ref[0] = (_dot(hb, wq_ref[...]) * (HEAD_DIM ** -0.5)).astype(BF16)
    g_ref[0] = _dot(hb, wg_ref[...])
    br_ref[0] = _dot(hb, wbr_ref[...])
    kv_ref[0] = _dot(hb, wkv_ref[...])


def _resident(shape):
    nd = len(shape)
    return pl.BlockSpec(shape, lambda *_: (0,) * nd, pipeline_mode=pl.Buffered(1))


def _proj_step(x, mods, g1, wu, wq, wkv, wg, wbr, *, ts):
    b, s, d = x.shape
    d_conv = wu.shape[1] // 2
    row = lambda n: pl.BlockSpec((1, ts, n), lambda bb, ss: (bb, ss, 0))
    widths = (d_conv, wq.shape[1], wkv.shape[1], wg.shape[1], wbr.shape[1])
    dtypes = (F32, BF16, F32, F32, F32)
    return pl.pallas_call(
        functools.partial(_proj_step_kernel, d_conv=d_conv),
        grid=(b, s // ts),
        in_specs=[row(d), _mod_spec(mods, 0, ts), _mod_spec(mods, 1, ts), _resident((1, d)),
                  _resident(wu.shape), _resident(wq.shape), _resident(wkv.shape),
                  _resident(wg.shape), _resident(wbr.shape)],
        out_specs=[row(n) for n in widths],
        out_shape=[jax.ShapeDtypeStruct((b, s, n), dt) for n, dt in zip(widths, dtypes)],
        compiler_params=_cparams(("arbitrary", "arbitrary")),
        name="in_proj_step",
    )(x, mods, mods, g1.reshape(1, d), wu, wq, wkv, wg, wbr)


def _proj_seq_kernel(x_ref, sh_ref, sc_ref, g1_ref, wu_ref, wqt_ref, wkvt_ref, wkn_ref, wgt_ref, wbr_ref,
                     u_ref, qt_ref, kn_ref, vt_ref, kv4t_ref, kvwt_ref, gt_ref, br_ref, *, d_conv):
    hb = _norm_mod(x_ref, sh_ref, sc_ref, g1_ref)
    _glu(hb, wu_ref, u_ref, d_conv)
    qt_ref[0] = (_dot_nt(wqt_ref[...], hb) * (HEAD_DIM ** -0.5 * LOG2E)).astype(BF16)
    gt_ref[0] = _dot_nt(wgt_ref[...], hb)
    br_ref[0] = _dot(hb, wbr_ref[...])
    kn_ref[0] = _dot(hb, wkn_ref[...]).astype(BF16)
    kvt = _dot_nt(wkvt_ref[...], hb)
    hd = HEAD_DIM
    n4 = 4 * N_KV_HEADS * hd
    kv4t_ref[0] = kvt[:n4]
    kvwt_ref[0] = kvt[n4:]
    for hh in range(N_KV_HEADS):
        vsel = (3 * N_KV_HEADS + hh) * hd
        vwin = (5 * N_KV_HEADS + hh) * hd
        vt_ref[0, hh * 2 * hd:hh * 2 * hd + hd] = kvt[vsel:vsel + hd].astype(BF16)
        vt_ref[0, hh * 2 * hd + hd:(hh + 1) * 2 * hd] = kvt[vwin:vwin + hd].astype(BF16)


def _proj_seq(x, mods, g1, wu, wqt, wkvt, wkn, wgt, wbr, *, ts):
    b, s, d = x.shape
    d_conv = wu.shape[1] // 2
    hd = HEAD_DIM
    row = lambda n: pl.BlockSpec((1, ts, n), lambda bb, ss: (bb, ss, 0))
    col = lambda n: pl.BlockSpec((1, n, ts), lambda bb, ss: (bb, 0, ss))
    nq, ng = wqt.shape[0], wgt.shape[0]
    n4, nw, nv = 4 * N_KV_HEADS * hd, 2 * N_KV_HEADS * hd, 2 * N_KV_HEADS * hd
    return pl.pallas_call(
        functools.partial(_proj_seq_kernel, d_conv=d_conv),
        grid=(b, s // ts),
        in_specs=[row(d), _mod_spec(mods, 0, ts), _mod_spec(mods, 1, ts), _resident((1, d)),
                  _resident(wu.shape), _resident(wqt.shape), _resident(wkvt.shape), _resident(wkn.shape),
                  _resident(wgt.shape), _resident(wbr.shape)],
        out_specs=[row(d_conv), col(nq), row(wkn.shape[1]), col(nv), col(n4), col(nw), col(ng), row(wbr.shape[1])],
        out_shape=[jax.ShapeDtypeStruct((b, s, d_conv), F32),
                   jax.ShapeDtypeStruct((b, nq, s), BF16),
                   jax.ShapeDtypeStruct((b, s, wkn.shape[1]), BF16),
                   jax.ShapeDtypeStruct((b, nv, s), BF16),
                   jax.ShapeDtypeStruct((b, n4, s), F32),
                   jax.ShapeDtypeStruct((b, nw, s), F32),
                   jax.ShapeDtypeStruct((b, ng, s), F32),
                   jax.ShapeDtypeStruct((b, s, wbr.shape[1]), F32)],
        compiler_params=_cparams(("arbitrary", "arbitrary")),
        name="in_proj_seq",
    )(x, mods, mods, g1.reshape(1, d), wu, wqt, wkvt, wkn, wgt, wbr)


def _ln_silu(y, g, b):
    mu = jnp.mean(y, axis=-1, keepdims=True)
    yc = y - mu
    var = jnp.mean(yc * yc, axis=-1, keepdims=True)
    return _silu(yc * lax.rsqrt(var + EPS) * g + b)


def _conv_seq_kernel(u_ref, halo_ref, w_ref, b_ref, lg_ref, lb_ref, o_ref, xs_ref, xsh_ref, acc_ref, *, ts, rb):
    i = pl.program_id(1)
    xs_ref[0:CONV_HALO] = jnp.where(i > 0, halo_ref[0], 0.0)
    xs_ref[CONV_HALO:CONV_HALO + ts] = u_ref[0]
    c = u_ref.shape[2]
    lane = LANE
    first = CONV_HALO - (CONV_K - 1)
    rows = CONV_HALO + ts - SUBLANE

    def col_body(ci, carry):
        c0 = pl.multiple_of(ci * lane, lane)
        for sh in range(1, SUBLANE):
            xsh_ref[sh, 0:rows, :] = xs_ref[pl.ds(sh, rows), pl.ds(c0, lane)]
        for r0 in range(0, ts, rb):
            acc = jnp.zeros((rb, lane), F32)
            for k in range(CONV_K):
                sh = (first + k) % SUBLANE
                base = r0 + first + k - sh
                if sh == 0:
                    x = xs_ref[pl.ds(base, rb), pl.ds(c0, lane)]
                else:
                    x = xsh_ref[sh, pl.ds(base, rb), :]
                acc = acc + w_ref[k:k + 1, pl.ds(c0, lane)] * x
            acc_ref[pl.ds(r0, rb), pl.ds(c0, lane)] = acc
        return carry

    lax.fori_loop(0, c // lane, col_body, 0)
    o_ref[0] = _ln_silu(acc_ref[...] + b_ref[...], lg_ref[...], lb_ref[...]).astype(o_ref.dtype)


def _conv_seq(u, w, b, lg, lb, *, ts):
    bsz, s, c = u.shape
    hb = ts // CONV_HALO
    vec = lambda: pl.BlockSpec((1, c), lambda bb, ss: (0, 0))
    return pl.pallas_call(
        functools.partial(_conv_seq_kernel, ts=ts, rb=32),
        grid=(bsz, s // ts),
        in_specs=[pl.BlockSpec((1, ts, c), lambda bb, ss: (bb, ss, 0)),
                  pl.BlockSpec((1, CONV_HALO, c), lambda bb, ss: (bb, jnp.maximum(ss * hb - 1, 0), 0)),
                  pl.BlockSpec((CONV_K, c), lambda bb, ss: (0, 0)), vec(), vec(), vec()],
        out_specs=pl.BlockSpec((1, ts, c), lambda bb, ss: (bb, ss, 0)),
        out_shape=jax.ShapeDtypeStruct((bsz, s, c), BF16),
        scratch_shapes=[pltpu.VMEM((CONV_HALO + ts, c), F32), pltpu.VMEM((SUBLANE, CONV_HALO + ts, LANE), F32),
                        pltpu.VMEM((ts, c), F32)],
        compiler_params=_cparams(("arbitrary", "arbitrary")),
        name="conv_seq",
    )(u, u, w, b.reshape(1, c), lg.reshape(1, c), lb.reshape(1, c))


def _conv_step_kernel(hist_ref, u_ref, w_ref, b_ref, lg_ref, lb_ref, o_ref):
    hist = hist_ref[:, 0]
    y = jnp.sum(hist * w_ref[0:CONV_K - 1][None], axis=1)
    y = y + u_ref[...] * w_ref[CONV_K - 1:CONV_K] + b_ref[...]
    o_ref[...] = _ln_silu(y, lg_ref[...], lb_ref[...]).astype(o_ref.dtype)


def _conv_step(hist, u, w, b, lg, lb):
    db, c = u.shape
    gb = 16 if db % 16 == 0 else db
    vec = lambda: pl.BlockSpec((1, c), lambda i: (0, 0))
    return pl.pallas_call(
        _conv_step_kernel,
        grid=(db // gb,),
        in_specs=[pl.BlockSpec((gb, 1, CONV_K - 1, c), lambda i: (i, 0, 0, 0)),
                  pl.BlockSpec((gb, c), lambda i: (i, 0)),
                  pl.BlockSpec((CONV_K, c), lambda i: (0, 0)), vec(), vec(), vec()],
        out_specs=pl.BlockSpec((gb, c), lambda i: (i, 0)),
        out_shape=jax.ShapeDtypeStruct((db, c), BF16),
        compiler_params=_cparams(("arbitrary",)),
        name="conv_step",
    )(hist, u, w, b.reshape(1, c), lg.reshape(1, c), lb.reshape(1, c))


def _merge_kernel(ca_ref, no_ref, br_ref, x_ref, m2_ref, sh_ref, sc_ref, g2_ref,
                  wco_ref, wno_ref, wo_ref, x1_ref, h2_ref, *, nsa_t):
    d = x_ref.shape[2]
    ya = _dot(ca_ref[0], wco_ref[...])
    if nsa_t:
        yb = lax.dot_general(no_ref[0], wno_ref[...], (((0,), (0,)), ((), ())), preferred_element_type=F32)
    else:
        yb = _dot(no_ref[0], wno_ref[...])
    br = br_ref[0]
    mix = jax.nn.sigmoid(br[:, :d]) * ya + jax.nn.sigmoid(br[:, d:]) * yb
    z = _dot(mix.astype(BF16), wo_ref[...])
    x1 = x_ref[0] + m2_ref[0, 0] * z
    x1_ref[0] = x1
    r = lax.rsqrt(jnp.mean(x1 * x1, axis=-1, keepdims=True) + EPS)
    h = (x1 * r) * g2_ref[...]
    h2_ref[0] = (h * (1.0 + sc_ref[0, 0]) + sh_ref[0, 0]).astype(BF16)


def _merge(cact, nsa_o, g_br, x, mods, g2, wco, wno, wo, *, ts, nsa_t):
    b, s, d = x.shape
    row = lambda n: pl.BlockSpec((1, ts, n), lambda bb, ss: (bb, ss, 0))
    nsa_spec = pl.BlockSpec((1, nsa_o.shape[1], ts), lambda bb, ss: (bb, 0, ss)) if nsa_t else row(nsa_o.shape[2])
    return pl.pallas_call(
        functools.partial(_merge_kernel, nsa_t=nsa_t),
        grid=(b, s // ts),
        in_specs=[row(cact.shape[2]), nsa_spec, row(2 * d), row(d),
                  _mod_spec(mods, 2, ts), _mod_spec(mods, 3, ts), _mod_spec(mods, 4, ts),
                  _resident((1, d)), _resident(wco.shape), _resident(wno.shape), _resident(wo.shape)],
        out_specs=[row(d), row(d)],
        out_shape=[jax.ShapeDtypeStruct((b, s, d), F32), jax.ShapeDtypeStruct((b, s, d), BF16)],
        compiler_params=_cparams(("arbitrary", "arbitrary")),
        name="merge",
    )(cact, nsa_o, g_br, x, mods, mods, mods, g2.reshape(1, d), wco, wno, wo)


def _gelu_tanh(x):
    return 0.5 * x * (1.0 + jnp.tanh(math.sqrt(2.0 / math.pi) * (x + 0.044715 * (x * x * x))))


def _ffn_kernel(*refs, seq_mode, ts, fc):
    if seq_mode:
        (h2_ref, x1_ref, m5_ref, gf_ref, wup_ref, cw_ref, cb_ref, wd_ref,
         y_ref, zt_ref, zs_ref, carry_ref) = refs
    else:
        (h2_ref, x1_ref, m5_ref, gf_ref, wup_ref, cw_ref, cb_ref, wd_ref, h0_ref, h1_ref,
         y_ref, zt_ref) = refs
    si = pl.program_id(1)
    d_ff = wd_ref.shape[0]
    h2 = h2_ref[0]
    acc = None
    for f in range(d_ff // fc):
        halves = []
        for part in range(2):
            c0 = part * d_ff + f * fc
            z = _dot(h2, wup_ref[:, c0:c0 + fc])
            w0, w1, w2 = (cw_ref[j:j + 1, c0:c0 + fc] for j in range(FFN_K))
            if seq_mode:
                slot = 2 * f + part
                buf = 2 * (f % 2) + part
                zs_ref[buf, 0:8] = jnp.where(si > 0, carry_ref[slot], 0.0)
                zs_ref[buf, 8:8 + ts] = z
                carry_ref[slot] = z[ts - 8:ts]
                zt_ref[0, :, c0:c0 + fc] = z[ts - 8:ts]
                zc = w0 * zs_ref[buf, pl.ds(6, ts)] + w1 * zs_ref[buf, pl.ds(7, ts)] + w2 * z
            else:
                zt_ref[0, :, c0:c0 + fc] = z
                zc = w0 * h0_ref[:, c0:c0 + fc] + w1 * h1_ref[:, c0:c0 + fc] + w2 * z
            halves.append(zc + cb_ref[:, c0:c0 + fc])
        act = (_gelu_tanh(halves[0]) * halves[1]).astype(BF16)
        contrib = _dot(act, wd_ref[f * fc:(f + 1) * fc, :])
        acc = contrib if acc is None else acc + contrib
    x2 = x1_ref[0] + m5_ref[0, 0] * acc
    r = lax.rsqrt(jnp.mean(x2 * x2, axis=-1, keepdims=True) + EPS)
    y_ref[0] = (x2 * r) * gf_ref[...]


def _ffn(h2, x1, mods, g_final, w_up, cw, cb, w_down, hist, *, ts, fc):
    b, s, d = x1.shape
    d_ff = w_down.shape[0]
    nf = d_ff // fc
    seq_mode = hist is None
    row = lambda n: pl.BlockSpec((1, ts, n), lambda bb, ss: (bb, ss, 0))
    in_specs = [row(d), row(d), _mod_spec(mods, 5, ts), _resident((1, d)),
                _resident(w_up.shape), _resident(cw.shape), _resident((1, 2 * d_ff)), _resident(w_down.shape)]
    args = [h2, x1, mods, g_final.reshape(1, d), w_up, cw, cb.reshape(1, -1), w_down]
    zrows = 8 * (s // ts) if seq_mode else ts
    scratch = []
    if seq_mode:
        scratch = [pltpu.VMEM((4, ts + 8, fc), F32), pltpu.VMEM((2 * nf, 8, fc), F32)]
        zspec = pl.BlockSpec((1, 8, 2 * d_ff), lambda bb, ss: (bb, ss, 0))
    else:
        assert b == 1 and s == ts
        h0, h1 = hist
        in_specs += [pl.BlockSpec((ts, 2 * d_ff), lambda bb, ss: (0, 0))] * 2
        args += [h0, h1]
        zspec = pl.BlockSpec((1, ts, 2 * d_ff), lambda bb, ss: (bb, 0, 0))
    return pl.pallas_call(
        functools.partial(_ffn_kernel, seq_mode=seq_mode, ts=ts, fc=fc),
        grid=(b, s // ts),
        in_specs=in_specs,
        out_specs=[row(d), zspec],
        out_shape=[jax.ShapeDtypeStruct((b, s, d), F32),
                   jax.ShapeDtypeStruct((b, zrows, 2 * d_ff), F32)],
        scratch_shapes=scratch,
        compiler_params=_cparams(("arbitrary", "arbitrary")),
        name="ffn",
    )(*args)


def _t5_bucket(d):
    max_exact = N_BUCKETS // 2
    df = jnp.maximum(d, 1).astype(F32)
    large = max_exact + (jnp.log(df / max_exact) / math.log(MAX_DISTANCE / max_exact)
                         * (N_BUCKETS - max_exact)).astype(jnp.int32)
    large = jnp.minimum(large, N_BUCKETS - 1)
    return jnp.where(d < max_exact, d, large)


def _bias_kernel(rb_ref, o_ref, *, off, ostride, rs, cs, dmax, scale):
    hh = pl.program_id(0)
    o = pl.program_id(1)
    rows, cols = o_ref.shape[2], o_ref.shape[3]
    d = (off + o * ostride + rs * lax.broadcasted_iota(jnp.int32, (rows, cols), 0)
         + cs * lax.broadcasted_iota(jnp.int32, (rows, cols), 1))
    bucket = _t5_bucket(jnp.maximum(d, 0))
    val = jnp.zeros((rows, cols), F32)
    for bk in range(N_BUCKETS):
        val = jnp.where(bucket == bk, rb_ref[bk, hh], val)
    ok = d >= 0
    if dmax is not None:
        ok = ok & (d <= dmax)
    o_ref[0, 0] = jnp.where(ok, val * scale, NEG)


def _bias_table(rel_bias, n_o, rows, cols, *, off, ostride=0, rs, cs, dmax=None, grouped=False, scale=1.0):
    if grouped:
        out_spec = pl.BlockSpec((1, 1, rows, cols), lambda hh, o: (hh // GROUP, o, 0, hh % GROUP))
        out_shape = jax.ShapeDtypeStruct((N_KV_HEADS, n_o, rows, GROUP * cols), F32)
    else:
        out_spec = pl.BlockSpec((1, 1, rows, cols), lambda hh, o: (hh, o, 0, 0))
        out_shape = jax.ShapeDtypeStruct((N_HEADS, n_o, rows, cols), F32)
    return pl.pallas_call(
        functools.partial(_bias_kernel, off=off, ostride=ostride, rs=rs, cs=cs, dmax=dmax, scale=scale),
        grid=(N_HEADS, n_o),
        in_specs=[pl.BlockSpec(memory_space=pltpu.SMEM)],
        out_specs=out_spec,
        out_shape=out_shape,
        compiler_params=_cparams(("arbitrary", "arbitrary")),
        name="bias_table",
    )(rel_bias)


def _compress_rows(x, pe, w1b, w2b):
    r = x.shape[0]
    xb = (x.reshape(r // CMP_BLK, CMP_BLK, HEAD_DIM) + pe[None]).reshape(r, HEAD_DIM).astype(BF16)
    h = _silu(_dot(xb, w1b))
    hm = jnp.sum(h.reshape(r // CMP_BLK, CMP_BLK, HEAD_DIM), axis=1) * (1.0 / CMP_BLK)
    return _dot(hm.astype(BF16), w2b)


def _topk_axis0(score, n_top):
    ns = score.shape[0]
    js = lax.broadcasted_iota(jnp.int32, score.shape, 0)
    sel = jnp.zeros(score.shape, jnp.bool_)
    winners = []
    for _ in range(n_top):
        m = jnp.max(score, axis=0, keepdims=True)
        first = jnp.min(jnp.where(score == m, js, ns), axis=0, keepdims=True)
        hit = js == first
        sel = sel | hit
        score = jnp.where(hit, REMOVED, score)
        winners.append(first)
    return sel, winners


def _nsa_step_cmp_kernel(q_ref, past_ref, new_ref, pe_ref, w1_ref, w2_ref, bcp_ref, bcn_ref,
                         oc_ref, idx_ref, kn_ref, imp_ref, impt_ref, *, gb, past_len, n_top):
    hd = HEAD_DIM
    ncp = past_ref.shape[4]
    nnew = bcn_ref.shape[1]
    ratio = SEL_BLK // CMP_BLK
    new_pad = SEL_BLK
    rows = gb * N_KV_HEADS * new_pad
    cm_new = []
    for slot in range(2):
        r_in_blk = lax.broadcasted_iota(jnp.int32, (gb * N_KV_HEADS, new_pad, hd), 1)
        xnew = new_ref[:, slot * N_KV_HEADS:(slot + 1) * N_KV_HEADS, :].reshape(gb * N_KV_HEADS, 1, hd)
        x = jnp.where(r_in_blk == 0, xnew, 0.0).reshape(rows, hd)
        cm_new.append(_compress_rows(x, pe_ref[slot], w1_ref[slot].astype(BF16), w2_ref[slot].astype(BF16)))
    nb_new = new_pad // CMP_BLK
    kn_ref[...] = jnp.zeros(kn_ref.shape, F32)
    imp_ref[...] = jnp.zeros(imp_ref.shape, F32)
    for bi in range(gb):
        for k in range(N_KV_HEADS):
            r0 = (bi * N_KV_HEADS + k) * nb_new
            kn_ref[0, 0:nb_new] = cm_new[0][r0:r0 + nb_new]
            kn_ref[1, 0:nb_new] = cm_new[1][r0:r0 + nb_new]
            qg = q_ref[bi, k * GROUP:(k + 1) * GROUP, :]
            bias = jnp.concatenate([bcp_ref[k * GROUP:(k + 1) * GROUP, :],
                                    bcn_ref[k * GROUP:(k + 1) * GROUP, :]], axis=1)
            lc = jnp.concatenate([_dot(qg, past_ref[bi, 0, k].astype(BF16)),
                                  _dot_nt(qg, kn_ref[0].astype(BF16))], axis=1) + bias
            mask = bias > 0.5 * NEG
            m = jnp.max(lc, axis=-1, keepdims=True)
            e = jnp.where(mask, jnp.exp(lc - m), 0.0)
            p = e / jnp.maximum(jnp.sum(e, axis=-1, keepdims=True), 1e-30)
            pb = p.astype(BF16)
            oc_ref[bi, k * GROUP:(k + 1) * GROUP, :] = (_dot_nt(pb[:, :ncp], past_ref[bi, 1, k].astype(BF16))
                                                        + _dot(pb[:, ncp:], kn_ref[1].astype(BF16)))
            row = bi * N_KV_HEADS + k
            imp_ref[row:row + 1, :] = jnp.sum(p, axis=0, keepdims=True)
    impt_ref[...] = imp_ref[...].T
    nsr = (ncp + nnew) // ratio
    imps = impt_ref[pl.ds(0, nsr, stride=ratio), :]
    for rr in range(1, ratio):
        imps = imps + impt_ref[pl.ds(rr, nsr, stride=ratio), :]
    ns = (past_len + new_pad) // SEL_BLK
    js = lax.broadcasted_iota(jnp.int32, imps.shape, 0)
    qblk = past_len // SEL_BLK
    forced = (js == 0) | (js == qblk) | (js == qblk - 1)
    score = jnp.where(forced, FORCE_SCORE, jnp.where(js * SEL_BLK <= past_len, imps, -1.0))
    score = jnp.where(js < ns, score, REMOVED)
    _, winners = _topk_axis0(score, n_top)
    idx_ref[0] = jnp.concatenate(winners, axis=0)


def _nsa_step_cmp(q, cmp_past, kv_new01, pe, w1, w2, bias_cp, bias_cn, *, gb, past_len):
    db = q.shape[0]
    ncp = cmp_past.shape[4]
    nnew = bias_cn.shape[1]
    hd = HEAD_DIM
    ns = (past_len + SEL_BLK) // SEL_BLK
    n_top = min(N_SEL, ns)
    assert gb * N_KV_HEADS <= LANE
    full = lambda a: pl.BlockSpec(a.shape, lambda i: (0,) * a.ndim)
    return pl.pallas_call(
        functools.partial(_nsa_step_cmp_kernel, gb=gb, past_len=past_len, n_top=n_top),
        grid=(db // gb,),
        in_specs=[pl.BlockSpec((gb, N_HEADS, hd), lambda i: (i, 0, 0)),
                  pl.BlockSpec((gb, 2, N_KV_HEADS, hd, ncp), lambda i: (i, 0, 0, 0, 0)),
                  pl.BlockSpec((gb, 2 * N_KV_HEADS, hd), lambda i: (i, 0, 0)),
                  full(pe), full(w1), full(w2), full(bias_cp), full(bias_cn)],
        out_specs=[pl.BlockSpec((gb, N_HEADS, hd), lambda i: (i, 0, 0)),
                   pl.BlockSpec((1, n_top, LANE), lambda i: (i, 0, 0))],
        out_shape=[jax.ShapeDtypeStruct((db, N_HEADS, hd), F32),
                   jax.ShapeDtypeStruct((db // gb, n_top, LANE), jnp.int32)],
        scratch_shapes=[pltpu.VMEM((2, nnew, hd), F32), pltpu.VMEM((LANE, ncp + nnew), F32),
                        pltpu.VMEM((ncp + nnew, LANE), F32)],
        compiler_params=_cparams(("arbitrary",)),
        name="nsa_step_cmp",
    )(q, cmp_past, kv_new01, pe, w1, w2, bias_cp, bias_cn)


def _compress_t(xt, pet, w1t, w2t, pool):
    xb = (xt + pet).astype(BF16)
    h = _silu(_dot(w1t, xb))
    hm = _dot(h.astype(BF16), pool)
    return _dot(w2t, hm.astype(BF16))


def _pool_matrix(r):
    rows = lax.broadcasted_iota(jnp.int32, (r, r // CMP_BLK), 0) // CMP_BLK
    cols = lax.broadcasted_iota(jnp.int32, (r, r // CMP_BLK), 1)
    return jnp.where(rows == cols, 1.0 / CMP_BLK, 0.0).astype(BF16)


def _compress_seq_t_kernel(x_ref, pe_ref, w1_ref, w2_ref, pool_ref, on_ref, ot_ref):
    ct = _compress_t(x_ref[0], pe_ref[0], w1_ref[0], w2_ref[0], pool_ref[...])
    ot_ref[0, 0] = ct
    on_ref[0, 0] = ct.T


def _compress_seq_t(kv4t, pet, w1t, w2t, pool):
    b, _, s = kv4t.shape
    hd = HEAD_DIM
    n = 2 * N_KV_HEADS
    nc = s // CMP_BLK
    wspec = lambda: pl.BlockSpec((1, hd, hd), lambda bb, j: (j // N_KV_HEADS, 0, 0))
    return pl.pallas_call(
        _compress_seq_t_kernel,
        grid=(b, n),
        in_specs=[pl.BlockSpec((1, hd, s), lambda bb, j: (bb, j, 0)),
                  pl.BlockSpec((1, hd, s), lambda bb, j: (j // N_KV_HEADS, 0, 0)),
                  wspec(), wspec(), pl.BlockSpec(pool.shape, lambda bb, j: (0, 0))],
        out_specs=[pl.BlockSpec((1, 1, nc, hd), lambda bb, j: (bb, j, 0, 0)),
                   pl.BlockSpec((1, 1, hd, nc), lambda bb, j: (bb, j, 0, 0))],
        out_shape=[jax.ShapeDtypeStruct((b, n, nc, hd), F32), jax.ShapeDtypeStruct((b, n, hd, nc), F32)],
        compiler_params=_cparams(("arbitrary", "arbitrary")),
        name="compress_seq",
    )(kv4t, pet, w1t, w2t, pool)


V_ROWS = 80


def _flash_step_t(k_tile, qa, v_tile, bias, shift, m_ref, acc_ref, idx):
    s = _dot(k_tile, qa)
    if bias is not None:
        s = s + bias
    m_old = m_ref[idx]
    mx = jnp.max(s, axis=0, keepdims=True)
    if shift is not None:
        mx = mx + shift
    m_new = jnp.maximum(m_old, mx)
    alpha = jnp.exp2(m_old - m_new)
    ms = m_new if shift is None else m_new - shift
    p = jnp.exp2(s - ms).astype(BF16)
    acc_ref[idx] = alpha * acc_ref[idx] + _dot(v_tile, p)
    m_ref[idx] = m_new


def _nsa_seq_t_kernel(qt_ref, kn_ref, vt_ref, kc_ref, vct_ref, gt_ref, bc_ref, bs_ref, bw_ref, far_ref, o_ref,
                      kaug_ref, kwaug_ref, vs_ref, vw_ref, qa_ref, impt_ref, oc_ref, m_ref, acc_ref,
                      *, t, n_near, n_top):
    kh = pl.program_id(1)
    i = pl.program_id(2)
    s_len = kn_ref.shape[1]
    ns = s_len // SEL_BLK
    hd = HEAD_DIM

    @pl.when(i == 0)
    def _():
        blk = lax.broadcasted_iota(jnp.int32, (s_len, ns), 0) // SEL_BLK
        col = lax.broadcasted_iota(jnp.int32, (s_len, ns), 1)
        kaug_ref[:, 0:hd] = kn_ref[0, :, 0:hd]
        kaug_ref[:, hd:hd + ns] = jnp.where(blk == col, NEG, 0.0).astype(BF16)
        kwaug_ref[:, 0:hd] = kn_ref[0, :, hd:2 * hd]
        kwaug_ref[:, hd:hd + ns] = jnp.zeros((s_len, ns), BF16)
        row = lax.broadcasted_iota(jnp.int32, (V_ROWS - hd, s_len), 0)
        tail = jnp.where(row == 0, 1.0, 0.0).astype(BF16)
        vs_ref[0:hd] = vt_ref[0, 0:hd]
        vs_ref[hd:V_ROWS] = tail
        vw_ref[0:hd] = vt_ref[0, hd:2 * hd]
        vw_ref[hd:V_ROWS] = tail

    for g in range(GROUP):
        qa_ref[0:hd, g * t:(g + 1) * t] = qt_ref[0, g * hd:(g + 1) * hd, :]
    kc = kc_ref[0, 0].astype(BF16)
    vct = vct_ref[0, 0].astype(BF16)
    bias = bc_ref[0, 0]
    lc = _dot(kc, qa_ref[0:hd, :]) + bias
    mask = bias > 0.5 * NEG
    m = jnp.max(lc, axis=0, keepdims=True)
    e = jnp.where(mask, jnp.exp2(lc - m), 0.0)
    p = e / jnp.maximum(jnp.sum(e, axis=0, keepdims=True), 1e-30)
    oc_ref[...] = _dot(vct, p.astype(BF16))
    imp = p[:, 0:t]
    for g in range(1, GROUP):
        imp = imp + p[:, g * t:(g + 1) * t]
    ratio = SEL_BLK // CMP_BLK
    parts = []
    for c in range(t // LANE):
        impt_ref[c] = imp[:, c * LANE:(c + 1) * LANE]
        part = impt_ref[c, pl.ds(0, ns, stride=ratio), :]
        for rr in range(1, ratio):
            part = part + impt_ref[c, pl.ds(rr, ns, stride=ratio), :]
        parts.append(part)
    imps = jnp.concatenate(parts, axis=1)
    js = lax.broadcasted_iota(jnp.int32, (ns, t), 0)
    pos = i * t + lax.broadcasted_iota(jnp.int32, (ns, t), 1)
    qblk = pos // SEL_BLK
    forced = (js == 0) | (js == qblk) | (js == qblk - 1)
    score = jnp.where(forced, FORCE_SCORE, jnp.where(js * SEL_BLK <= pos, imps, -1.0))
    sel, _ = _topk_axis0(score, n_top)
    notsel = jnp.where(sel, 0.0, 1.0).astype(BF16)
    for g in range(GROUP):
        qa_ref[hd:hd + ns, g * t:(g + 1) * t] = notsel

    m_ref[...] = jnp.full(m_ref.shape, NEG, F32)
    acc_ref[...] = jnp.zeros(acc_ref.shape, F32)
    n_far = jnp.maximum(i - (n_near - 1), 0)
    far_shift = jnp.concatenate([jnp.full((1, t), far_ref[kh * GROUP + g], F32) for g in range(GROUP)], axis=1)

    def sel_step(r0, rows, bias, shift):
        _flash_step_t(kaug_ref[pl.ds(r0, rows), :], qa_ref[...], vs_ref[:, pl.ds(r0, rows)],
                      bias, shift, m_ref, acc_ref, 0)

    def far_body(jj, carry):
        sel_step(pl.multiple_of(jj * 4 * t, 4 * t), 4 * t, None, far_shift)
        return carry

    lax.fori_loop(0, n_far // 4, far_body, 0)

    @pl.when(n_far % 4 >= 2)
    def _():
        sel_step(pl.multiple_of((n_far // 4) * 4 * t, 2 * t), 2 * t, None, far_shift)

    @pl.when(n_far % 2 == 1)
    def _():
        sel_step(pl.multiple_of((n_far - 1) * t, t), t, None, far_shift)

    def tiled(step, bias_ref, n):
        def body(j, carry):
            r = pl.multiple_of((n - 1 - (i - j)) * t, t)
            step(pl.multiple_of(j * t, t), t, bias_ref[0, pl.ds(r, t), :])
            return carry
        lax.fori_loop(0, i + 1, body, 0)

    @pl.when(i >= n_near - 1)
    def _():
        sel_step(pl.multiple_of((i - (n_near - 1)) * t, t), n_near * t, bs_ref[0], None)

    @pl.when(i < n_near - 1)
    def _():
        tiled(lambda r0, rows, bias: sel_step(r0, rows, bias, None), bs_ref, n_near)

    n_wt = WINDOW // t + 1

    def win_step(r0, rows, bias):
        _flash_step_t(kwaug_ref[pl.ds(r0, rows), :], qa_ref[...], vw_ref[:, pl.ds(r0, rows)],
                      bias, None, m_ref, acc_ref, 1)

    @pl.when(i >= n_wt - 1)
    def _():
        win_step(pl.multiple_of((i - (n_wt - 1)) * t, t), n_wt * t, bw_ref[0])

    @pl.when(i < n_wt - 1)
    def _():
        tiled(win_step, bw_ref, n_wt)

    def gate(r):
        rows = [gt_ref[0, pl.ds((kh * GROUP + g) * 3 + r, 1), :] for g in range(GROUP)]
        return jax.nn.sigmoid(jnp.concatenate(rows, axis=1))

    acc_s = acc_ref[0]
    acc_w = acc_ref[1]
    o_s = acc_s[0:hd] / acc_s[hd:hd + 1]
    o_w = acc_w[0:hd] / acc_w[hd:hd + 1]
    o = gate(0) * oc_ref[...] + gate(1) * o_s + gate(2) * o_w
    for g in range(GROUP):
        o_ref[0, g * hd:(g + 1) * hd, :] = o[:, g * t:(g + 1) * t].astype(o_ref.dtype)


def _nsa_seq_t(qt, kn, vt, cmp_n, cmp_t, gt, rel_bias, *, t):
    b, nq, s = qt.shape
    hd = HEAD_DIM
    ns = s // SEL_BLK
    nc = s // CMP_BLK
    n_top = min(N_SEL, ns)
    assert ns == hd and s % t == 0 and t % LANE == 0 and WINDOW % t == 0
    n_near = min(-(-(MAX_DISTANCE + t - 1) // t), s // t)
    n_wt = WINDOW // t + 1
    assert n_wt <= n_near + 1
    bias_s = _bias_table(rel_bias, n_near, t, t, off=(n_near - 1) * t, ostride=-t, rs=-1, cs=1, grouped=True,
                         scale=LOG2E)
    bias_s = bias_s.reshape(N_KV_HEADS, n_near * t, GROUP * t)
    bias_w = _bias_table(rel_bias, n_wt, t, t, off=(n_wt - 1) * t, ostride=-t, rs=-1, cs=1, dmax=WINDOW,
                         grouped=True, scale=LOG2E)
    bias_w = bias_w.reshape(N_KV_HEADS, n_wt * t, GROUP * t)
    bias_c = _bias_table(rel_bias, s // t, nc, t, off=-(CMP_BLK - 1), ostride=t, rs=-CMP_BLK, cs=1, grouped=True,
                         scale=LOG2E)
    far = rel_bias[N_BUCKETS - 1] * LOG2E
    gw = GROUP * hd
    gt_ = GROUP * t
    return pl.pallas_call(
        functools.partial(_nsa_seq_t_kernel, t=t, n_near=n_near, n_top=n_top),
        grid=(b, N_KV_HEADS, s // t),
        in_specs=[pl.BlockSpec((1, gw, t), lambda bb, k, i: (bb, k, i)),
                  pl.BlockSpec((1, s, 2 * hd), lambda bb, k, i: (bb, 0, k)),
                  pl.BlockSpec((1, 2 * hd, s), lambda bb, k, i: (bb, k, 0)),
                  pl.BlockSpec((1, 1, nc, hd), lambda bb, k, i: (bb, k, 0, 0)),
                  pl.BlockSpec((1, 1, hd, nc), lambda bb, k, i: (bb, N_KV_HEADS + k, 0, 0)),
                  pl.BlockSpec((1, gt.shape[1], t), lambda bb, k, i: (bb, 0, i)),
                  pl.BlockSpec((1, 1, nc, gt_), lambda bb, k, i: (k, i, 0, 0)),
                  pl.BlockSpec((1, n_near * t, gt_), lambda bb, k, i: (k, 0, 0)),
                  pl.BlockSpec((1, n_wt * t, gt_), lambda bb, k, i: (k, 0, 0)),
                  pl.BlockSpec(memory_space=pltpu.SMEM)],
        out_specs=pl.BlockSpec((1, gw, t), lambda bb, k, i: (bb, k, i)),
        out_shape=jax.ShapeDtypeStruct((b, nq, s), BF16),
        scratch_shapes=[pltpu.VMEM((s, 2 * hd), BF16), pltpu.VMEM((s, 2 * hd), BF16),
                        pltpu.VMEM((V_ROWS, s), BF16), pltpu.VMEM((V_ROWS, s), BF16),
                        pltpu.VMEM((2 * hd, gt_), BF16), pltpu.VMEM((t // LANE, nc, LANE), F32),
                        pltpu.VMEM((hd, gt_), F32),
                        pltpu.VMEM((2, 1, gt_), F32), pltpu.VMEM((2, V_ROWS, gt_), F32)],
        compiler_params=_cparams(("arbitrary", "arbitrary", "arbitrary")),
        name="nsa_seq",
    )(qt, kn, vt, cmp_n, cmp_t, gt, bias_c, bias_s, bias_w, far)


def _compress_pages_t_kernel(pt_ref, *refs, pg):
    page_refs = refs[:pg]
    pe_ref, w1_ref, w2_ref, pool_ref, o_ref = refs[pg:]
    for slot in range(2):
        for hh in range(N_KV_HEADS):
            xt = jnp.concatenate([page_refs[n][0, 0, slot, hh] for n in range(pg)], axis=1)
            o_ref[0, slot, hh] = _compress_t(xt, pe_ref[slot], w1_ref[slot], w2_ref[slot], pool_ref[...])


def _compress_pages_t(cache_t, page_table, layer, pet, w1t, w2t, pool, *, pg):
    db, n_pages = page_table.shape
    bpp = PAGE_SIZE // CMP_BLK
    hd = HEAD_DIM

    def page_map(b, p, pt, n):
        return (pt[b * n_pages + p * pg + n], layer, 0, 0, 0, 0)

    full = lambda a: pl.BlockSpec(a.shape, lambda b, p, pt: (0,) * a.ndim)
    grid_spec = pltpu.PrefetchScalarGridSpec(
        num_scalar_prefetch=1,
        grid=(db, n_pages // pg),
        in_specs=[pl.BlockSpec((1, 1, 2, N_KV_HEADS, hd, PAGE_SIZE), functools.partial(page_map, n=n))
                  for n in range(pg)] + [full(pet), full(w1t), full(w2t), full(pool)],
        out_specs=pl.BlockSpec((1, 2, N_KV_HEADS, hd, pg * bpp), lambda b, p, pt: (b, 0, 0, 0, p)),
    )
    return pl.pallas_call(
        functools.partial(_compress_pages_t_kernel, pg=pg),
        grid_spec=grid_spec,
        out_shape=jax.ShapeDtypeStruct((db, 2, N_KV_HEADS, hd, n_pages * bpp), F32),
        compiler_params=_cparams(("arbitrary", "arbitrary")),
        name="compress_pages",
    )(page_table.reshape(-1), *([cache_t] * pg), pet, w1t, w2t, pool)


def _nsa_step_sel_t_kernel(idx_ref, pt_ref, *refs, n_top, nsp):
    blk_refs = refs[:n_top]
    (q_ref, win_ref, new_ref, newt_ref, oc_ref, g_ref, bsel_ref, bwin_ref, o_ref, ks_ref, vs_ref, bs_ref) = refs[n_top:]
    b = pl.program_id(0)
    k = pl.program_id(1)
    hd = HEAD_DIM
    bpp = PAGE_SIZE // SEL_BLK
    qg = q_ref[0, 0]
    new = new_ref[0, 0]
    newt = newt_ref[0, 0]
    lane = lax.broadcasted_iota(jnp.int32, (hd, PAGE_SIZE), 1)
    k_newblk = jnp.where(lane == 0, newt[:, 2:3], 0.0)
    v_newblk = jnp.where(lane == 0, newt[:, 3:4], 0.0)
    for n in range(n_top):
        idn = idx_ref[(b * N_KV_HEADS + k) * n_top + n]
        is_new = idn >= nsp
        half = jnp.where(is_new, 0, jnp.minimum(idn, nsp - 1) % bpp)
        ks_ref[:, n * PAGE_SIZE:(n + 1) * PAGE_SIZE] = jnp.where(is_new, k_newblk, blk_refs[n][0, 0, 0, 0]).astype(BF16)
        vs_ref[:, n * PAGE_SIZE:(n + 1) * PAGE_SIZE] = jnp.where(is_new, v_newblk, blk_refs[n][0, 0, 1, 0]).astype(BF16)
        for g in range(GROUP):
            brow = bsel_ref[g, 0, pl.ds(idn, 1), :]
            for hf in range(bpp):
                c0 = n * PAGE_SIZE + hf * SEL_BLK
                bs_ref[g:g + 1, c0:c0 + SEL_BLK] = jnp.where(half == hf, brow, NEG)
    bias = bs_ref[...]
    ls = _dot(qg, ks_ref[...]) + bias
    mask = bias > 0.5 * NEG
    m = jnp.max(ls, axis=-1, keepdims=True)
    e = jnp.where(mask, jnp.exp(ls - m), 0.0)
    p = e / jnp.maximum(jnp.sum(e, axis=-1, keepdims=True), 1e-30)
    o_s = _dot_nt(p.astype(BF16), vs_ref[...])
    wb = win_ref.shape[5]
    bw = bwin_ref[0]
    bias_p = bw[:, 0:wb]
    bias_n = bw[:, wb:wb + 1]
    kn = new[4:5].astype(BF16).astype(F32)
    vn = new[5:6].astype(BF16).astype(F32)
    lw = _dot(qg, win_ref[0, 0, 0, 0].astype(BF16)) + bias_p
    lwn = jnp.sum(qg.astype(F32) * kn, axis=-1, keepdims=True) + bias_n
    mask_p = bias_p > 0.5 * NEG
    mask_n = bias_n > 0.5 * NEG
    m = jnp.maximum(jnp.max(lw, axis=-1, keepdims=True), lwn)
    e_p = jnp.where(mask_p, jnp.exp(lw - m), 0.0)
    e_n = jnp.where(mask_n, jnp.exp(lwn - m), 0.0)
    den = jnp.maximum(jnp.sum(e_p, axis=-1, keepdims=True) + e_n, 1e-30)
    o_w = (_dot_nt((e_p / den).astype(BF16), win_ref[0, 0, 1, 0].astype(BF16))
           + (e_n / den).astype(BF16).astype(F32) * vn)
    gates = jnp.broadcast_to(jax.nn.sigmoid(g_ref[0]), (GROUP, g_ref.shape[2]))
    col = lax.broadcasted_iota(jnp.int32, gates.shape, 1)
    head = k * GROUP + lax.broadcasted_iota(jnp.int32, gates.shape, 0)
    gate = lambda r: jnp.sum(jnp.where(col == head * 3 + r, gates, 0.0), axis=-1, keepdims=True)
    o_ref[0, 0] = gate(0) * oc_ref[0, 0] + gate(1) * o_s + gate(2) * o_w


def _nsa_step_sel_t(idx, page_table, cache_t, layer, q, win_t, kv_new, kv_new_t, o_c, g_nsa, bias_sel, bias_win,
                    *, past_len):
    db, n_pages = page_table.shape
    n_top = idx.shape[-1]
    hd = HEAD_DIM
    nsp = past_len // SEL_BLK
    bpp = PAGE_SIZE // SEL_BLK
    wb = win_t.shape[5]

    def blk_map(b, k, ix, pt, n):
        jp = jnp.minimum(ix[(b * N_KV_HEADS + k) * n_top + n], nsp - 1)
        return (pt[b * n_pages + jp // bpp], layer, 1, k, 0, 0)

    grid_spec = pltpu.PrefetchScalarGridSpec(
        num_scalar_prefetch=2,
        grid=(db, N_KV_HEADS),
        in_specs=[pl.BlockSpec((1, 1, 2, 1, hd, PAGE_SIZE), functools.partial(blk_map, n=n)) for n in range(n_top)]
        + [pl.BlockSpec((1, 1, GROUP, hd), lambda b, k, ix, pt: (b, k, 0, 0)),
           pl.BlockSpec((1, 1, 2, 1, hd, wb), lambda b, k, ix, pt: (b, layer, 0, k, 0, 0)),
           pl.BlockSpec((1, 1, 6, hd), lambda b, k, ix, pt: (b, k, 0, 0)),
           pl.BlockSpec((1, 1, hd, 8), lambda b, k, ix, pt: (b, k, 0, 0)),
           pl.BlockSpec((1, 1, GROUP, hd), lambda b, k, ix, pt: (b, k, 0, 0)),
           pl.BlockSpec((1, 1, g_nsa.shape[-1]), lambda b, k, ix, pt: (b, 0, 0)),
           pl.BlockSpec((GROUP, 1) + bias_sel.shape[2:], lambda b, k, ix, pt: (k, 0, 0, 0)),
           pl.BlockSpec((1, GROUP, bias_win.shape[-1]), lambda b, k, ix, pt: (k, 0, 0))],
        out_specs=pl.BlockSpec((1, 1, GROUP, hd), lambda b, k, ix, pt: (b, k, 0, 0)),
        scratch_shapes=[pltpu.VMEM((hd, n_top * PAGE_SIZE), BF16), pltpu.VMEM((hd, n_top * PAGE_SIZE), BF16),
                        pltpu.VMEM((GROUP, n_top * PAGE_SIZE), F32)],
    )
    return pl.pallas_call(
        functools.partial(_nsa_step_sel_t_kernel, n_top=n_top, nsp=nsp),
        grid_spec=grid_spec,
        out_shape=jax.ShapeDtypeStruct((db, N_KV_HEADS, GROUP, hd), F32),
        compiler_params=_cparams(("arbitrary", "arbitrary")),
        name="nsa_step_sel",
    )(idx.reshape(-1), page_table.reshape(-1), *([cache_t] * n_top), q, win_t, kv_new, kv_new_t, o_c, g_nsa,
      bias_sel, bias_win)


def _pick(n, pref):
    for c in pref:
        if n % c == 0:
            return c
    return n


def kernel(x_prompt, x_sample, cache_kv, state_win_kv, state_conv, state_ffn, page_table, c_prompt, c_sample,
           w_ada, b_ada, g_norm1, g_norm2, w_in, conv_w, conv_b, conv_ln_g, conv_ln_b, w_conv_out,
           cmp_pe, cmp_w1, cmp_w2, w_nsa_out, w_out, w_up, ffn_conv_w, ffn_conv_b, w_down, rel_bias, g_final):
    depth = w_ada.shape[0]
    assert depth == 1
    layer = 0
    b, s, d = x_prompt.shape
    db = x_sample.shape[0]
    assert x_sample.shape[1] == 1
    hd = HEAD_DIM
    d_conv = conv_w.shape[2]
    d_ff = w_down.shape[1]
    nq = N_HEADS * hd
    nkv = 6 * N_KV_HEADS * hd
    past_len = page_table.shape[1] * PAGE_SIZE
    wb = state_win_kv.shape[4]

    o0, o1, o2, o3 = 2 * d_conv, 2 * d_conv + nq, 2 * d_conv + nq + nkv, 2 * d_conv + nq + nkv + 3 * N_HEADS
    wi = w_in[layer]
    wu = wi[:, :o0].astype(BF16)
    wq = wi[:, o0:o1].astype(BF16)
    wkv = wi[:, o1:o2].astype(BF16)
    wkv6 = wkv.reshape(d, 6, N_KV_HEADS, hd)
    wkn = jnp.stack([wkv6[:, 2], wkv6[:, 4]], axis=2).reshape(d, 2 * N_KV_HEADS * hd)
    wg = jnp.pad(wi[:, o2:o3], ((0, 0), (0, 128 - 3 * N_HEADS))).astype(BF16)
    wbr = wi[:, o3:].astype(BF16)
    wco = w_conv_out[layer].astype(BF16)
    wno = w_nsa_out[layer].astype(BF16)
    wo = w_out[layer].astype(BF16)
    wup = w_up[layer].astype(BF16)
    wdn = w_down[layer].astype(BF16)

    mod = _ada(jnp.concatenate([c_prompt, c_sample], axis=0), w_ada[layer], b_ada[layer])
    mod = mod.reshape(b + db, 6, d)
    mods_p = mod[:b].transpose(1, 0, 2).reshape(6, b, 1, d)
    mods_s = mod[b:].transpose(1, 0, 2).reshape(6, 1, db, d)

    ts = _pick(s, (256, 128))
    u_p, qt_p, kn_p, vt_p, kv4t_p, kvwt_p, gt_p, br_p = _proj_seq(
        x_prompt, mods_p, g_norm1[layer], wu, wq.T, wkv.T, wkn, wg.T, wbr, ts=ts)
    cact_p = _conv_seq(u_p, conv_w[layer], conv_b[layer], conv_ln_g[layer], conv_ln_b[layer], ts=ts)
    pe_t = cmp_pe[layer].transpose(0, 2, 1)
    w1t = cmp_w1[layer].transpose(0, 2, 1).astype(BF16)
    w2t = cmp_w2[layer].transpose(0, 2, 1).astype(BF16)
    cmpn_p, cmpt_p = _compress_seq_t(kv4t_p, jnp.tile(pe_t, (1, 1, s // CMP_BLK)), w1t, w2t, _pool_matrix(s))
    ot_p = _nsa_seq_t(qt_p, kn_p, vt_p, cmpn_p, cmpt_p, gt_p, rel_bias, t=256)
    x1_p, h2_p = _merge(cact_p, ot_p, br_p, x_prompt, mods_p, g_norm2[layer], wco, wno, wo,
                        ts=_pick(s, (512, 256)), nsa_t=True)
    y_prompt, zt_p = _ffn(h2_p, x1_p, mods_p, g_final, wup, ffn_conv_w[layer], ffn_conv_b[layer], wdn, None,
                              ts=_pick(s, (512, 256)), fc=512)
    w_len = min(WINDOW, s)
    kv_prompt = jnp.swapaxes(kv4t_p.reshape(b, 1, 4, N_KV_HEADS, hd, s), -1, -2)
    win_prompt = jnp.swapaxes(kvwt_p.reshape(b, 1, 2, N_KV_HEADS, hd, s)[..., s - w_len:], -1, -2)
    conv_prompt = u_p[:, None, s - (CONV_K - 1):]
    ffn_prompt = zt_p[:, None, zt_p.shape[1] - (FFN_K - 1):]

    xs = x_sample.reshape(1, db, d)
    u_s, q_s, kv_s, g_s, br_s = _proj_step(xs, mods_s, g_norm1[layer], wu, wq, wkv, wg, wbr, ts=db)
    kv_s = kv_s.reshape(db, 6, N_KV_HEADS, hd)
    cact_s = _conv_step(state_conv[:, layer:layer + 1], u_s[0], conv_w[layer], conv_b[layer],
                        conv_ln_g[layer], conv_ln_b[layer])
    cache_t = jnp.swapaxes(cache_kv, -1, -2)
    win_t = jnp.swapaxes(state_win_kv, -1, -2)
    pg = LANE // (PAGE_SIZE // CMP_BLK)
    assert page_table.shape[1] % pg == 0
    rp = pg * PAGE_SIZE
    cmp_past = _compress_pages_t(cache_t, page_table, layer, jnp.tile(pe_t, (1, 1, rp // CMP_BLK)), w1t, w2t,
                                 _pool_matrix(rp), pg=pg)
    ncp = cmp_past.shape[4]
    nnew = 128
    bias_cp = _bias_table(rel_bias, 1, 8, ncp, off=past_len - (CMP_BLK - 1), rs=0, cs=-CMP_BLK)[:, 0, 0]
    bias_cn = _bias_table(rel_bias, 1, 8, nnew, off=past_len - (CMP_BLK - 1) - ncp * CMP_BLK, rs=0, cs=-CMP_BLK)[:, 0, 0]
    gb = _pick(db, (8,))
    oc_s, idx_t = _nsa_step_cmp(q_s.reshape(db, N_HEADS, hd), cmp_past, kv_s[:, :2].reshape(db, 2 * N_KV_HEADS, hd),
                                cmp_pe[layer], cmp_w1[layer], cmp_w2[layer], bias_cp, bias_cn, gb=gb,
                                past_len=past_len)
    n_top = idx_t.shape[1]
    idx = idx_t[:, :, :gb * N_KV_HEADS].reshape(db // gb, n_top, gb, N_KV_HEADS).transpose(0, 2, 3, 1).reshape(db, N_KV_HEADS, n_top)
    ns_tot = (past_len + SEL_BLK) // SEL_BLK
    ns_rows = -(-ns_tot // 8) * 8
    bias_sel = _bias_table(rel_bias, 1, ns_rows, SEL_BLK, off=past_len, rs=-SEL_BLK, cs=-1)
    wcols = -(-(wb + 1) // 128) * 128
    bias_win = _bias_table(rel_bias, 1, 8, wcols, off=wb, rs=0, cs=-1, dmax=WINDOW)[:, 0, 0]
    bias_win = bias_win.reshape(N_KV_HEADS, GROUP, wcols)
    kv_new = kv_s.transpose(0, 2, 1, 3)
    kv_new_t = jnp.pad(kv_s.transpose(0, 2, 3, 1), ((0, 0), (0, 0), (0, 0), (0, 2)))
    o_s = _nsa_step_sel_t(idx, page_table, cache_t, layer, q_s.reshape(db, N_KV_HEADS, GROUP, hd), win_t,
                          kv_new, kv_new_t, oc_s.reshape(db, N_KV_HEADS, GROUP, hd),
                          g_s.reshape(db, 1, -1), bias_sel, bias_win, past_len=past_len)
    o_s = o_s.reshape(1, db, nq).astype(BF16)
    x1_s, h2_s = _merge(cact_s.reshape(1, db, d_conv), o_s, br_s, xs, mods_s, g_norm2[layer], wco, wno, wo,
                        ts=db, nsa_t=False)
    hist_f = state_ffn[:, layer]
    y_s, z_s = _ffn(h2_s, x1_s, mods_s, g_final, wup, ffn_conv_w[layer], ffn_conv_b[layer], wdn,
                           (hist_f[:, 0], hist_f[:, 1]), ts=db, fc=512)
    y_sample = y_s.reshape(db, 1, d)
    kv_sample = kv_s[:, :4].reshape(db, 1, 4, N_KV_HEADS, 1, hd)
    win_sample = jnp.concatenate([state_win_kv[:, layer, :, :, 1:], kv_s[:, 4:, :, None, :]], axis=3)[:, None]
    conv_sample = jnp.concatenate([state_conv[:, layer, 1:], u_s[0][:, None, :]], axis=1)[:, None]
    z_s = z_s[0]
    ffn_sample = jnp.concatenate([hist_f[:, 1:], z_s[:, None, :]], axis=1)[:, None]
    return (y_prompt, y_sample, kv_prompt, kv_sample, win_prompt, win_sample, conv_prompt, conv_sample,
            ffn_prompt, ffn_sample)
```

```python
import functools
import math

import jax
import jax.numpy as jnp
from jax import lax
from jax.experimental import pallas as pl
from jax.experimental.pallas import tpu as pltpu

F32 = jnp.float32
BF16 = jnp.bfloat16

N_HEADS = 16
HEAD_DIM = 64
N_KV_HEADS = 4
GROUP = N_HEADS // N_KV_HEADS
CMP_BLK = 32
SEL_BLK = 64
N_SEL = 16
WINDOW = 512
N_BUCKETS = 32
MAX_DISTANCE = 1024
CONV_K = 31
FFN_K = 3
PAGE_SIZE = 128
EPS = 1e-6
LOG2E = math.log2(math.e)
FORCE_SCORE = 1e4
NEG = -1e30
REMOVED = -3e38
LANE = 128
SUBLANE = 8
CONV_HALO = 32
VMEM_LIMIT = 56 * 1024 * 1024


def _cparams(sem):
    return pltpu.CompilerParams(dimension_semantics=sem, vmem_limit_bytes=VMEM_LIMIT)


def _dot(a, b):
    return jnp.dot(a, b, preferred_element_type=F32)


def _dot_nt(a, b):
    return lax.dot_general(a, b, (((1,), (1,)), ((), ())), preferred_element_type=F32)


def _silu(x):
    return x * jax.nn.sigmoid(x)


def _ada_kernel(c_ref, w_ref, b_ref, o_ref):
    s = _silu(c_ref[...]).astype(BF16)
    o_ref[...] = _dot(s, w_ref[...].astype(BF16)) + b_ref[...]


def _ada(c, w, b):
    m, d = c.shape
    n = w.shape[1]
    tn = 1536 if n % 1536 == 0 else n
    return pl.pallas_call(
        _ada_kernel,
        grid=(n // tn,),
        in_specs=[pl.BlockSpec((m, d), lambda j: (0, 0)),
                  pl.BlockSpec((d, tn), lambda j: (0, j)),
                  pl.BlockSpec((1, tn), lambda j: (0, j))],
        out_specs=pl.BlockSpec((m, tn), lambda j: (0, j)),
        out_shape=jax.ShapeDtypeStruct((m, n), F32),
        compiler_params=_cparams(("arbitrary",)),
        name="ada_mod",
    )(c, w, b.reshape(1, n))


def _mod_spec(mods, i, ts):
    d = mods.shape[-1]
    if mods.shape[2] == 1:
        return pl.BlockSpec((1, 1, 1, d), lambda b, s, *_: (i, b, 0, 0))
    return pl.BlockSpec((1, 1, ts, d), lambda b, s, *_: (i, b, s, 0))


def _norm_mod(x_ref, sh_ref, sc_ref, g_ref):
    x = x_ref[0]
    r = lax.rsqrt(jnp.mean(x * x, axis=-1, keepdims=True) + EPS)
    h = (x * r) * g_ref[...]
    return (h * (1.0 + sc_ref[0, 0]) + sh_ref[0, 0]).astype(BF16)


def _glu(hb, wu_ref, u_ref, d_conv):
    cw = 256
    for c in range(d_conv // cw):
        a = _dot(hb, wu_ref[:, c * cw:(c + 1) * cw])
        g = _dot(hb, wu_ref[:, d_conv + c * cw:d_conv + (c + 1) * cw])
        u_ref[0, :, c * cw:(c + 1) * cw] = a * jax.nn.sigmoid(g)


def _proj_step_kernel(x_ref, sh_ref, sc_ref, g1_ref, wu_ref, wq_ref, wkv_ref, wg_ref, wbr_ref,
                      u_ref, q_ref, kv_ref, g_ref, br_ref, *, d_conv):
    hb = _norm_mod(x_ref, sh_ref, sc_ref, g1_ref)
    _glu(hb, wu_ref, u_ref, d_conv)
    q_ref[0] = (_dot(hb, wq_ref[...]) * (HEAD_DIM ** -0.5)).astype(BF16)
    g_ref[0] = _dot(hb, wg_ref[...])
    br_ref[0] = _dot(hb, wbr_ref[...])
    kv_ref[0] = _dot(hb, wkv_ref[...])


def _resident(shape):
    nd = len(shape)
    return pl.BlockSpec(shape, lambda *_: (0,) * nd, pipeline_mode=pl.Buffered(1))


def _proj_step(x, mods, g1, wu, wq, wkv, wg, wbr, *, ts):
    b, s, d = x.shape
    d_conv = wu.shape[1] // 2
    row = lambda n: pl.BlockSpec((1, ts, n), lambda bb, ss: (bb, ss, 0))
    widths = (d_conv, wq.shape[1], wkv.shape[1], wg.shape[1], wbr.shape[1])
    dtypes = (F32, BF16, F32, F32, F32)
    return pl.pallas_call(
        functools.partial(_proj_step_kernel, d_conv=d_conv),
        grid=(b, s // ts),
        in_specs=[row(d), _mod_spec(mods, 0, ts), _mod_spec(mods, 1, ts), _resident((1, d)),
                  _resident(wu.shape), _resident(wq.shape), _resident(wkv.shape),
                  _resident(wg.shape), _resident(wbr.shape)],
        out_specs=[row(n) for n in widths],
        out_shape=[jax.ShapeDtypeStruct((b, s, n), dt) for n, dt in zip(widths, dtypes)],
        compiler_params=_cparams(("arbitrary", "arbitrary")),
        name="in_proj_step",
    )(x, mods, mods, g1.reshape(1, d), wu, wq, wkv, wg, wbr)


def _proj_seq_kernel(x_ref, sh_ref, sc_ref, g1_ref, wu_ref, wqt_ref, wkvt_ref, wkn_ref, wgt_ref, wbr_ref,
                     u_ref, qt_ref, kn_ref, vt_ref, kv4t_ref, kvwt_ref, gt_ref, br_ref, *, d_conv):
    hb = _norm_mod(x_ref, sh_ref, sc_ref, g1_ref)
    _glu(hb, wu_ref, u_ref, d_conv)
    qt_ref[0] = (_dot_nt(wqt_ref[...], hb) * (HEAD_DIM ** -0.5 * LOG2E)).astype(BF16)
    gt_ref[0] = _dot_nt(wgt_ref[...], hb)
    br_ref[0] = _dot(hb, wbr_ref[...])
    kn_ref[0] = _dot(hb, wkn_ref[...]).astype(BF16)
    kvt = _dot_nt(wkvt_ref[...], hb)
    hd = HEAD_DIM
    n4 = 4 * N_KV_HEADS * hd
    kv4t_ref[0] = kvt[:n4]
    kvwt_ref[0] = kvt[n4:]
    for hh in range(N_KV_HEADS):
        vsel = (3 * N_KV_HEADS + hh) * hd
        vwin = (5 * N_KV_HEADS + hh) * hd
        vt_ref[0, hh * 2 * hd:hh * 2 * hd + hd] = kvt[vsel:vsel + hd].astype(BF16)
        vt_ref[0, hh * 2 * hd + hd:(hh + 1) * 2 * hd] = kvt[vwin:vwin + hd].astype(BF16)


def _proj_seq(x, mods, g1, wu, wqt, wkvt, wkn, wgt, wbr, *, ts):
    b, s, d = x.shape
    d_conv = wu.shape[1] // 2
    hd = HEAD_DIM
    row = lambda n: pl.BlockSpec((1, ts, n), lambda bb, ss: (bb, ss, 0))
    col = lambda n: pl.BlockSpec((1, n, ts), lambda bb, ss: (bb, 0, ss))
    nq, ng = wqt.shape[0], wgt.shape[0]
    n4, nw, nv = 4 * N_KV_HEADS * hd, 2 * N_KV_HEADS * hd, 2 * N_KV_HEADS * hd
    return pl.pallas_call(
        functools.partial(_proj_seq_kernel, d_conv=d_conv),
        grid=(b, s // ts),
        in_specs=[row(d), _mod_spec(mods, 0, ts), _mod_spec(mods, 1, ts), _resident((1, d)),
                  _resident(wu.shape), _resident(wqt.shape), _resident(wkvt.shape), _resident(wkn.shape),
                  _resident(wgt.shape), _resident(wbr.shape)],
        out_specs=[row(d_conv), col(nq), row(wkn.shape[1]), col(nv), col(n4), col(nw), col(ng), row(wbr.shape[1])],
        out_shape=[jax.ShapeDtypeStruct((b, s, d_conv), F32),
                   jax.ShapeDtypeStruct((b, nq, s), BF16),
                   jax.ShapeDtypeStruct((b, s, wkn.shape[1]), BF16),
                   jax.ShapeDtypeStruct((b, nv, s), BF16),
                   jax.ShapeDtypeStruct((b, n4, s), F32),
                   jax.ShapeDtypeStruct((b, nw, s), F32),
                   jax.ShapeDtypeStruct((b, ng, s), F32),
                   jax.ShapeDtypeStruct((b, s, wbr.shape[1]), F32)],
        compiler_params=_cparams(("arbitrary", "arbitrary")),
        name="in_proj_seq",
    )(x, mods, mods, g1.reshape(1, d), wu, wqt, wkvt, wkn, wgt, wbr)


def _ln_silu(y, g, b):
    mu = jnp.mean(y, axis=-1, keepdims=True)
    yc = y - mu
    var = jnp.mean(yc * yc, axis=-1, keepdims=True)
    return _silu(yc * lax.rsqrt(var + EPS) * g + b)


def _conv_seq_kernel(u_ref, halo_ref, w_ref, b_ref, lg_ref, lb_ref, o_ref, xs_ref, xsh_ref, acc_ref, *, ts, rb):
    i = pl.program_id(1)
    xs_ref[0:CONV_HALO] = jnp.where(i > 0, halo_ref[0], 0.0)
    xs_ref[CONV_HALO:CONV_HALO + ts] = u_ref[0]
    c = u_ref.shape[2]
    lane = LANE
    first = CONV_HALO - (CONV_K - 1)
    rows = CONV_HALO + ts - SUBLANE

    def col_body(ci, carry):
        c0 = pl.multiple_of(ci * lane, lane)
        for sh in range(1, SUBLANE):
            xsh_ref[sh, 0:rows, :] = xs_ref[pl.ds(sh, rows), pl.ds(c0, lane)]
        for r0 in range(0, ts, rb):
            acc = jnp.zeros((rb, lane), F32)
            for k in range(CONV_K):
                sh = (first + k) % SUBLANE
                base = r0 + first + k - sh
                if sh == 0:
                    x = xs_ref[pl.ds(base, rb), pl.ds(c0, lane)]
                else:
                    x = xsh_ref[sh, pl.ds(base, rb), :]
                acc = acc + w_ref[k:k + 1, pl.ds(c0, lane)] * x
            acc_ref[pl.ds(r0, rb), pl.ds(c0, lane)] = acc
        return carry

    lax.fori_loop(0, c // lane, col_body, 0)
    o_ref[0] = _ln_silu(acc_ref[...] + b_ref[...], lg_ref[...], lb_ref[...]).astype(o_ref.dtype)


def _conv_seq(u, w, b, lg, lb, *, ts):
    bsz, s, c = u.shape
    hb = ts // CONV_HALO
    vec = lambda: pl.BlockSpec((1, c), lambda bb, ss: (0, 0))
    return pl.pallas_call(
        functools.partial(_conv_seq_kernel, ts=ts, rb=32),
        grid=(bsz, s // ts),
        in_specs=[pl.BlockSpec((1, ts, c), lambda bb, ss: (bb, ss, 0)),
                  pl.BlockSpec((1, CONV_HALO, c), lambda bb, ss: (bb, jnp.maximum(ss * hb - 1, 0), 0)),
                  pl.BlockSpec((CONV_K, c), lambda bb, ss: (0, 0)), vec(), vec(), vec()],
        out_specs=pl.BlockSpec((1, ts, c), lambda bb, ss: (bb, ss, 0)),
        out_shape=jax.ShapeDtypeStruct((bsz, s, c), BF16),
        scratch_shapes=[pltpu.VMEM((CONV_HALO + ts, c), F32), pltpu.VMEM((SUBLANE, CONV_HALO + ts, LANE), F32),
                        pltpu.VMEM((ts, c), F32)],
        compiler_params=_cparams(("arbitrary", "arbitrary")),
        name="conv_seq",
    )(u, u, w, b.reshape(1, c), lg.reshape(1, c), lb.reshape(1, c))


def _conv_step_kernel(hist_ref, u_ref, w_ref, b_ref, lg_ref, lb_ref, o_ref):
    hist = hist_ref[:, 0]
    y = jnp.sum(hist * w_ref[0:CONV_K - 1][None], axis=1)
    y = y + u_ref[...] * w_ref[CONV_K - 1:CONV_K] + b_ref[...]
    o_ref[...] = _ln_silu(y, lg_ref[...], lb_ref[...]).astype(o_ref.dtype)


def _conv_step(hist, u, w, b, lg, lb):
    db, c = u.shape
    gb = 16 if db % 16 == 0 else db
    vec = lambda: pl.BlockSpec((1, c), lambda i: (0, 0))
    return pl.pallas_call(
        _conv_step_kernel,
        grid=(db // gb,),
        in_specs=[pl.BlockSpec((gb, 1, CONV_K - 1, c), lambda i: (i, 0, 0, 0)),
                  pl.BlockSpec((gb, c), lambda i: (i, 0)),
                  pl.BlockSpec((CONV_K, c), lambda i: (0, 0)), vec(), vec(), vec()],
        out_specs=pl.BlockSpec((gb, c), lambda i: (i, 0)),
        out_shape=jax.ShapeDtypeStruct((db, c), BF16),
        compiler_params=_cparams(("arbitrary",)),
        name="conv_step",
    )(hist, u, w, b.reshape(1, c), lg.reshape(1, c), lb.reshape(1, c))


def _merge_kernel(ca_ref, no_ref, br_ref, x_ref, m2_ref, sh_ref, sc_ref, g2_ref,
                  wco_ref, wno_ref, wo_ref, x1_ref, h2_ref, *, nsa_t):
    d = x_ref.shape[2]
    ya = _dot(ca_ref[0], wco_ref[...])
    if nsa_t:
        yb = lax.dot_general(no_ref[0], wno_ref[...], (((0,), (0,)), ((), ())), preferred_element_type=F32)
    else:
        yb = _dot(no_ref[0], wno_ref[...])
    br = br_ref[0]
    mix = jax.nn.sigmoid(br[:, :d]) * ya + jax.nn.sigmoid(br[:, d:]) * yb
    z = _dot(mix.astype(BF16), wo_ref[...])
    x1 = x_ref[0] + m2_ref[0, 0] * z
    x1_ref[0] = x1
    r = lax.rsqrt(jnp.mean(x1 * x1, axis=-1, keepdims=True) + EPS)
    h = (x1 * r) * g2_ref[...]
    h2_ref[0] = (h * (1.0 + sc_ref[0, 0]) + sh_ref[0, 0]).astype(BF16)


def _merge(cact, nsa_o, g_br, x, mods, g2, wco, wno, wo, *, ts, nsa_t):
    b, s, d = x.shape
    row = lambda n: pl.BlockSpec((1, ts, n), lambda bb, ss: (bb, ss, 0))
    nsa_spec = pl.BlockSpec((1, nsa_o.shape[1], ts), lambda bb, ss: (bb, 0, ss)) if nsa_t else row(nsa_o.shape[2])
    return pl.pallas_call(
        functools.partial(_merge_kernel, nsa_t=nsa_t),
        grid=(b, s // ts),
        in_specs=[row(cact.shape[2]), nsa_spec, row(2 * d), row(d),
                  _mod_spec(mods, 2, ts), _mod_spec(mods, 3, ts), _mod_spec(mods, 4, ts),
                  _resident((1, d)), _resident(wco.shape), _resident(wno.shape), _resident(wo.shape)],
        out_specs=[row(d), row(d)],
        out_shape=[jax.ShapeDtypeStruct((b, s, d), F32), jax.ShapeDtypeStruct((b, s, d), BF16)],
        compiler_params=_cparams(("arbitrary", "arbitrary")),
        name="merge",
    )(cact, nsa_o, g_br, x, mods, mods, mods, g2.reshape(1, d), wco, wno, wo)


def _gelu_tanh(x):
    return 0.5 * x * (1.0 + jnp.tanh(math.sqrt(2.0 / math.pi) * (x + 0.044715 * (x * x * x))))


def _ffn_kernel(*refs, seq_mode, ts, fc):
    if seq_mode:
        (h2_ref, x1_ref, m5_ref, gf_ref, wup_ref, cw_ref, cb_ref, wd_ref,
         y_ref, zt_ref, zs_ref, carry_ref) = refs
    else:
        (h2_ref, x1_ref, m5_ref, gf_ref, wup_ref, cw_ref, cb_ref, wd_ref, h0_ref, h1_ref,
         y_ref, zt_ref) = refs
    si = pl.program_id(1)
    d_ff = wd_ref.shape[0]
    h2 = h2_ref[0]
    acc = None
    for f in range(d_ff // fc):
        halves = []
        for part in range(2):
            c0 = part * d_ff + f * fc
            z = _dot(h2, wup_ref[:, c0:c0 + fc])
            w0, w1, w2 = (cw_ref[j:j + 1, c0:c0 + fc] for j in range(FFN_K))
            if seq_mode:
                slot = 2 * f + part
                buf = 2 * (f % 2) + part
                zs_ref[buf, 0:8] = jnp.where(si > 0, carry_ref[slot], 0.0)
                zs_ref[buf, 8:8 + ts] = z
                carry_ref[slot] = z[ts - 8:ts]
                zt_ref[0, :, c0:c0 + fc] = z[ts - 8:ts]
                zc = w0 * zs_ref[buf, pl.ds(6, ts)] + w1 * zs_ref[buf, pl.ds(7, ts)] + w2 * z
            else:
                zt_ref[0, :, c0:c0 + fc] = z
                zc = w0 * h0_ref[:, c0:c0 + fc] + w1 * h1_ref[:, c0:c0 + fc] + w2 * z
            halves.append(zc + cb_ref[:, c0:c0 + fc])
        act = (_gelu_tanh(halves[0]) * halves[1]).astype(BF16)
        contrib = _dot(act, wd_ref[f * fc:(f + 1) * fc, :])
        acc = contrib if acc is None else acc + contrib
    x2 = x1_ref[0] + m5_ref[0, 0] * acc
    r = lax.rsqrt(jnp.mean(x2 * x2, axis=-1, keepdims=True) + EPS)
    y_ref[0] = (x2 * r) * gf_ref[...]


def _ffn(h2, x1, mods, g_final, w_up, cw, cb, w_down, hist, *, ts, fc):
    b, s, d = x1.shape
    d_ff = w_down.shape[0]
    nf = d_ff // fc
    seq_mode = hist is None
    row = lambda n: pl.BlockSpec((1, ts, n), lambda bb, ss: (bb, ss, 0))
    in_specs = [row(d), row(d), _mod_spec(mods, 5, ts), _resident((1, d)),
                _resident(w_up.shape), _resident(cw.shape), _resident((1, 2 * d_ff)), _resident(w_down.shape)]
    args = [h2, x1, mods, g_final.reshape(1, d), w_up, cw, cb.reshape(1, -1), w_down]
    zrows = 8 * (s // ts) if seq_mode else ts
    scratch = []
    if seq_mode:
        scratch = [pltpu.VMEM((4, ts + 8, fc), F32), pltpu.VMEM((2 * nf, 8, fc), F32)]
        zspec = pl.BlockSpec((1, 8, 2 * d_ff), lambda bb, ss: (bb, ss, 0))
    else:
        assert b == 1 and s == ts
        h0, h1 = hist
        in_specs += [pl.BlockSpec((ts, 2 * d_ff), lambda bb, ss: (0, 0))] * 2
        args += [h0, h1]
        zspec = pl.BlockSpec((1, ts, 2 * d_ff), lambda bb, ss: (bb, 0, 0))
    return pl.pallas_call(
        functools.partial(_ffn_kernel, seq_mode=seq_mode, ts=ts, fc=fc),
        grid=(b, s // ts),
        in_specs=in_specs,
        out_specs=[row(d), zspec],
        out_shape=[jax.ShapeDtypeStruct((b, s, d), F32),
                   jax.ShapeDtypeStruct((b, zrows, 2 * d_ff), F32)],
        scratch_shapes=scratch,
        compiler_params=_cparams(("arbitrary", "arbitrary")),
        name="ffn",
    )(*args)


def _t5_bucket(d):
    max_exact = N_BUCKETS // 2
    df = jnp.maximum(d, 1).astype(F32)
    large = max_exact + (jnp.log(df / max_exact) / math.log(MAX_DISTANCE / max_exact)
                         * (N_BUCKETS - max_exact)).astype(jnp.int32)
    large = jnp.minimum(large, N_BUCKETS - 1)
    return jnp.where(d < max_exact, d, large)


def _bias_kernel(rb_ref, o_ref, *, off, ostride, rs, cs, dmax, scale):
    hh = pl.program_id(0)
    o = pl.program_id(1)
    rows, cols = o_ref.shape[2], o_ref.shape[3]
    d = (off + o * ostride + rs * lax.broadcasted_iota(jnp.int32, (rows, cols), 0)
         + cs * lax.broadcasted_iota(jnp.int32, (rows, cols), 1))
    bucket = _t5_bucket(jnp.maximum(d, 0))
    val = jnp.zeros((rows, cols), F32)
    for bk in range(N_BUCKETS):
        val = jnp.where(bucket == bk, rb_ref[bk, hh], val)
    ok = d >= 0
    if dmax is not None:
        ok = ok & (d <= dmax)
    o_ref[0, 0] = jnp.where(ok, val * scale, NEG)


def _bias_table(rel_bias, n_o, rows, cols, *, off, ostride=0, rs, cs, dmax=None, grouped=False, scale=1.0):
    if grouped:
        out_spec = pl.BlockSpec((1, 1, rows, cols), lambda hh, o: (hh // GROUP, o, 0, hh % GROUP))
        out_shape = jax.ShapeDtypeStruct((N_KV_HEADS, n_o, rows, GROUP * cols), F32)
    else:
        out_spec = pl.BlockSpec((1, 1, rows, cols), lambda hh, o: (hh, o, 0, 0))
        out_shape = jax.ShapeDtypeStruct((N_HEADS, n_o, rows, cols), F32)
    return pl.pallas_call(
        functools.partial(_bias_kernel, off=off, ostride=ostride, rs=rs, cs=cs, dmax=dmax, scale=scale),
        grid=(N_HEADS, n_o),
        in_specs=[pl.BlockSpec(memory_space=pltpu.SMEM)],
        out_specs=out_spec,
        out_shape=out_shape,
        compiler_params=_cparams(("arbitrary", "arbitrary")),
        name="bias_table",
    )(rel_bias)


def _compress_rows(x, pe, w1b, w2b):
    r = x.shape[0]
    xb = (x.reshape(r // CMP_BLK, CMP_BLK, HEAD_DIM) + pe[None]).reshape(r, HEAD_DIM).astype(BF16)
    h = _silu(_dot(xb, w1b))
    hm = jnp.sum(h.reshape(r // CMP_BLK, CMP_BLK, HEAD_DIM), axis=1) * (1.0 / CMP_BLK)
    return _dot(hm.astype(BF16), w2b)


def _topk_axis0(score, n_top):
    ns = score.shape[0]
    js = lax.broadcasted_iota(jnp.int32, score.shape, 0)
    sel = jnp.zeros(score.shape, jnp.bool_)
    winners = []
    for _ in range(n_top):
        m = jnp.max(score, axis=0, keepdims=True)
        first = jnp.min(jnp.where(score == m, js, ns), axis=0, keepdims=True)
        hit = js == first
        sel = sel | hit
        score = jnp.where(hit, REMOVED, score)
        winners.append(first)
    return sel, winners


def _nsa_step_cmp_kernel(q_ref, past_ref, new_ref, pe_ref, w1_ref, w2_ref, bcp_ref, bcn_ref,
                         oc_ref, idx_ref, kn_ref, imp_ref, impt_ref, *, gb, past_len, n_top):
    hd = HEAD_DIM
    ncp = past_ref.shape[4]
    nnew = bcn_ref.shape[1]
    ratio = SEL_BLK // CMP_BLK
    new_pad = SEL_BLK
    rows = gb * N_KV_HEADS * new_pad
    cm_new = []
    for slot in range(2):
        r_in_blk = lax.broadcasted_iota(jnp.int32, (gb * N_KV_HEADS, new_pad, hd), 1)
        xnew = new_ref[:, slot * N_KV_HEADS:(slot + 1) * N_KV_HEADS, :].reshape(gb * N_KV_HEADS, 1, hd)
        x = jnp.where(r_in_blk == 0, xnew, 0.0).reshape(rows, hd)
        cm_new.append(_compress_rows(x, pe_ref[slot], w1_ref[slot].astype(BF16), w2_ref[slot].astype(BF16)))
    nb_new = new_pad // CMP_BLK
    kn_ref[...] = jnp.zeros(kn_ref.shape, F32)
    imp_ref[...] = jnp.zeros(imp_ref.shape, F32)
    for bi in range(gb):
        for k in range(N_KV_HEADS):
            r0 = (bi * N_KV_HEADS + k) * nb_new
            kn_ref[0, 0:nb_new] = cm_new[0][r0:r0 + nb_new]
            kn_ref[1, 0:nb_new] = cm_new[1][r0:r0 + nb_new]
            qg = q_ref[bi, k * GROUP:(k + 1) * GROUP, :]
            bias = jnp.concatenate([bcp_ref[k * GROUP:(k + 1) * GROUP, :],
                                    bcn_ref[k * GROUP:(k + 1) * GROUP, :]], axis=1)
            lc = jnp.concatenate([_dot(qg, past_ref[bi, 0, k].astype(BF16)),
                                  _dot_nt(qg, kn_ref[0].astype(BF16))], axis=1) + bias
            mask = bias > 0.5 * NEG
            m = jnp.max(lc, axis=-1, keepdims=True)
            e = jnp.where(mask, jnp.exp(lc - m), 0.0)
            p = e / jnp.maximum(jnp.sum(e, axis=-1, keepdims=True), 1e-30)
            pb = p.astype(BF16)
            oc_ref[bi, k * GROUP:(k + 1) * GROUP, :] = (_dot_nt(pb[:, :ncp], past_ref[bi, 1, k].astype(BF16))
                                                        + _dot(pb[:, ncp:], kn_ref[1].astype(BF16)))
            row = bi * N_KV_HEADS + k
            imp_ref[row:row + 1, :] = jnp.sum(p, axis=0, keepdims=True)
    impt_ref[...] = imp_ref[...].T
    nsr = (ncp + nnew) // ratio
    imps = impt_ref[pl.ds(0, nsr, stride=ratio), :]
    for rr in range(1, ratio):
        imps = imps + impt_ref[pl.ds(rr, nsr, stride=ratio), :]
    ns = (past_len + new_pad) // SEL_BLK
    js = lax.broadcasted_iota(jnp.int32, imps.shape, 0)
    qblk = past_len // SEL_BLK
    forced = (js == 0) | (js == qblk) | (js == qblk - 1)
    score = jnp.where(forced, FORCE_SCORE, jnp.where(js * SEL_BLK <= past_len, imps, -1.0))
    score = jnp.where(js < ns, score, REMOVED)
    _, winners = _topk_axis0(score, n_top)
    idx_ref[0] = jnp.concatenate(winners, axis=0)


def _nsa_step_cmp(q, cmp_past, kv_new01, pe, w1, w2, bias_cp, bias_cn, *, gb, past_len):
    db = q.shape[0]
    ncp = cmp_past.shape[4]
    nnew = bias_cn.shape[1]
    hd = HEAD_DIM
    ns = (past_len + SEL_BLK) // SEL_BLK
    n_top = min(N_SEL, ns)
    assert gb * N_KV_HEADS <= LANE
    full = lambda a: pl.BlockSpec(a.shape, lambda i: (0,) * a.ndim)
    return pl.pallas_call(
        functools.partial(_nsa_step_cmp_kernel, gb=gb, past_len=past_len, n_top=n_top),
        grid=(db // gb,),
        in_specs=[pl.BlockSpec((gb, N_HEADS, hd), lambda i: (i, 0, 0)),
                  pl.BlockSpec((gb, 2, N_KV_HEADS, hd, ncp), lambda i: (i, 0, 0, 0, 0)),
                  pl.BlockSpec((gb, 2 * N_KV_HEADS, hd), lambda i: (i, 0, 0)),
                  full(pe), full(w1), full(w2), full(bias_cp), full(bias_cn)],
        out_specs=[pl.BlockSpec((gb, N_HEADS, hd), lambda i: (i, 0, 0)),
                   pl.BlockSpec((1, n_top, LANE), lambda i: (i, 0, 0))],
        out_shape=[jax.ShapeDtypeStruct((db, N_HEADS, hd), F32),
                   jax.ShapeDtypeStruct((db // gb, n_top, LANE), jnp.int32)],
        scratch_shapes=[pltpu.VMEM((2, nnew, hd), F32), pltpu.VMEM((LANE, ncp + nnew), F32),
                        pltpu.VMEM((ncp + nnew, LANE), F32)],
        compiler_params=_cparams(("arbitrary",)),
        name="nsa_step_cmp",
    )(q, cmp_past, kv_new01, pe, w1, w2, bias_cp, bias_cn)


def _compress_t(xt, pet, w1t, w2t, pool):
    xb = (xt + pet).astype(BF16)
    h = _silu(_dot(w1t, xb))
    hm = _dot(h.astype(BF16), pool)
    return _dot(w2t, hm.astype(BF16))


def _pool_matrix(r):
    rows = lax.broadcasted_iota(jnp.int32, (r, r // CMP_BLK), 0) // CMP_BLK
    cols = lax.broadcasted_iota(jnp.int32, (r, r // CMP_BLK), 1)
    return jnp.where(rows == cols, 1.0 / CMP_BLK, 0.0).astype(BF16)


def _compress_seq_t_kernel(x_ref, pe_ref, w1_ref, w2_ref, pool_ref, on_ref, ot_ref):
    ct = _compress_t(x_ref[0], pe_ref[0], w1_ref[0], w2_ref[0], pool_ref[...])
    ot_ref[0, 0] = ct
    on_ref[0, 0] = ct.T


def _compress_seq_t(kv4t, pet, w1t, w2t, pool):
    b, _, s = kv4t.shape
    hd = HEAD_DIM
    n = 2 * N_KV_HEADS
    nc = s // CMP_BLK
    wspec = lambda: pl.BlockSpec((1, hd, hd), lambda bb, j: (j // N_KV_HEADS, 0, 0))
    return pl.pallas_call(
        _compress_seq_t_kernel,
        grid=(b, n),
        in_specs=[pl.BlockSpec((1, hd, s), lambda bb, j: (bb, j, 0)),
                  pl.BlockSpec((1, hd, s), lambda bb, j: (j // N_KV_HEADS, 0, 0)),
                  wspec(), wspec(), pl.BlockSpec(pool.shape, lambda bb, j: (0, 0))],
        out_specs=[pl.BlockSpec((1, 1, nc, hd), lambda bb, j: (bb, j, 0, 0)),
                   pl.BlockSpec((1, 1, hd, nc), lambda bb, j: (bb, j, 0, 0))],
        out_shape=[jax.ShapeDtypeStruct((b, n, nc, hd), F32), jax.ShapeDtypeStruct((b, n, hd, nc), F32)],
        compiler_params=_cparams(("arbitrary", "arbitrary")),
        name="compress_seq",
    )(kv4t, pet, w1t, w2t, pool)


FLASH_CHUNK = 256
V_ROWS = 80


def _flash_step_t(k_tile, qa, v_tile, bias, shift, m_ref, acc_ref, idx):
    rows = k_tile.shape[0]
    ck = FLASH_CHUNK
    ss = []
    for c in range(rows // ck):
        s = _dot(k_tile[c * ck:(c + 1) * ck], qa)
        if bias is not None:
            s = s + bias[c * ck:(c + 1) * ck]
        ss.append(s)
    m = m_ref[idx]
    acc = acc_ref[idx]
    for c, s in enumerate(ss):
        mx = jnp.max(s, axis=0, keepdims=True)
        if shift is not None:
            mx = mx + shift
        m_new = jnp.maximum(m, mx)
        alpha = jnp.exp2(m - m_new)
        ms = m_new if shift is None else m_new - shift
        p = jnp.exp2(s - ms).astype(BF16)
        acc = alpha * acc + _dot(v_tile[:, c * ck:(c + 1) * ck], p)
        m = m_new
    acc_ref[idx] = acc
    m_ref[idx] = m


def _nsa_seq_t_kernel(qt_ref, kn_ref, vt_ref, kc_ref, vct_ref, gt_ref, bc_ref, bs_ref, bw_ref, far_ref, o_ref,
                      kaug_ref, kwaug_ref, vs_ref, vw_ref, qa_ref, impt_ref, oc_ref, m_ref, acc_ref,
                      *, t, n_near, n_top):
    kh = pl.program_id(1)
    i = pl.program_id(2)
    s_len = kn_ref.shape[1]
    ns = s_len // SEL_BLK
    hd = HEAD_DIM

    @pl.when(i == 0)
    def _():
        blk = lax.broadcasted_iota(jnp.int32, (s_len, ns), 0) // SEL_BLK
        col = lax.broadcasted_iota(jnp.int32, (s_len, ns), 1)
        kaug_ref[:, 0:hd] = kn_ref[0, :, 0:hd]
        kaug_ref[:, hd:hd + ns] = jnp.where(blk == col, NEG, 0.0).astype(BF16)
        kwaug_ref[:, 0:hd] = kn_ref[0, :, hd:2 * hd]
        kwaug_ref[:, hd:hd + ns] = jnp.zeros((s_len, ns), BF16)
        row = lax.broadcasted_iota(jnp.int32, (V_ROWS - hd, s_len), 0)
        tail = jnp.where(row == 0, 1.0, 0.0).astype(BF16)
        vs_ref[0:hd] = vt_ref[0, 0:hd]
        vs_ref[hd:V_ROWS] = tail
        vw_ref[0:hd] = vt_ref[0, hd:2 * hd]
        vw_ref[hd:V_ROWS] = tail

    for g in range(GROUP):
        qa_ref[0:hd, g * t:(g + 1) * t] = qt_ref[0, g * hd:(g + 1) * hd, :]
    kc = kc_ref[0, 0].astype(BF16)
    vct = vct_ref[0, 0].astype(BF16)
    bias = bc_ref[0, 0]
    lc = _dot(kc, qa_ref[0:hd, :]) + bias
    mask = bias > 0.5 * NEG
    m = jnp.max(lc, axis=0, keepdims=True)
    e = jnp.where(mask, jnp.exp2(lc - m), 0.0)
    p = e / jnp.maximum(jnp.sum(e, axis=0, keepdims=True), 1e-30)
    oc_ref[...] = _dot(vct, p.astype(BF16))
    imp = p[:, 0:t]
    for g in range(1, GROUP):
        imp = imp + p[:, g * t:(g + 1) * t]
    ratio = SEL_BLK // CMP_BLK
    parts = []
    for c in range(t // LANE):
        impt_ref[c] = imp[:, c * LANE:(c + 1) * LANE]
        part = impt_ref[c, pl.ds(0, ns, stride=ratio), :]
        for rr in range(1, ratio):
            part = part + impt_ref[c, pl.ds(rr, ns, stride=ratio), :]
        parts.append(part)
    imps = jnp.concatenate(parts, axis=1)
    js = lax.broadcasted_iota(jnp.int32, (ns, t), 0)
    pos = i * t + lax.broadcasted_iota(jnp.int32, (ns, t), 1)
    qblk = pos // SEL_BLK
    forced = (js == 0) | (js == qblk) | (js == qblk - 1)
    score = jnp.where(forced, FORCE_SCORE, jnp.where(js * SEL_BLK <= pos, imps, -1.0))
    sel, _ = _topk_axis0(score, n_top)
    notsel = jnp.where(sel, 0.0, 1.0).astype(BF16)
    for g in range(GROUP):
        qa_ref[hd:hd + ns, g * t:(g + 1) * t] = notsel

    m_ref[...] = jnp.full(m_ref.shape, NEG, F32)
    acc_ref[...] = jnp.zeros(acc_ref.shape, F32)
    n_far = jnp.maximum(i - (n_near - 1), 0)
    far_shift = jnp.concatenate([jnp.full((1, t), far_ref[kh * GROUP + g], F32) for g in range(GROUP)], axis=1)

    def sel_step(r0, rows, bias, shift):
        _flash_step_t(kaug_ref[pl.ds(r0, rows), :], qa_ref[...], vs_ref[:, pl.ds(r0, rows)],
                      bias, shift, m_ref, acc_ref, 0)

    def tiled(step, bias_ref, n):
        def body(j, carry):
            r = pl.multiple_of((n - 1 - (i - j)) * t, t)
            step(pl.multiple_of(j * t, t), t, bias_ref[0, pl.ds(r, t), :])
            return carry
        lax.fori_loop(0, i + 1, body, 0)

    n_wt = WINDOW // t + 1

    def near_step(r0, rows, bias):
        sel_step(r0, rows, bias, None)

    def win_step(r0, rows, bias):
        _flash_step_t(kwaug_ref[pl.ds(r0, rows), :], qa_ref[...], vw_ref[:, pl.ds(r0, rows)],
                      bias, None, m_ref, acc_ref, 1)

    def far_body(jj, carry):
        sel_step(pl.multiple_of(jj * 8 * t, 8 * t), 8 * t, None, far_shift)
        return carry

    lax.fori_loop(0, n_far // 8, far_body, 0)
    for w in (4, 2, 1):
        @pl.when((n_far // w) % 2 == 1)
        def _(w=w):
            sel_step(pl.multiple_of((n_far // (2 * w)) * 2 * w * t, w * t), w * t, None, far_shift)

    @pl.when(i >= n_near - 1)
    def _():
        near_step(pl.multiple_of((i - (n_near - 1)) * t, t), n_near * t, bs_ref[0])

    @pl.when(i < n_near - 1)
    def _():
        tiled(near_step, bs_ref, n_near)

    @pl.when(i >= n_wt - 1)
    def _():
        win_step(pl.multiple_of((i - (n_wt - 1)) * t, t), n_wt * t, bw_ref[0])

    @pl.when(i < n_wt - 1)
    def _():
        tiled(win_step, bw_ref, n_wt)

    def gate(r):
        rows = [gt_ref[0, pl.ds((kh * GROUP + g) * 3 + r, 1), :] for g in range(GROUP)]
        return jax.nn.sigmoid(jnp.concatenate(rows, axis=1))

    acc_s = acc_ref[0]
    acc_w = acc_ref[1]
    o_s = acc_s[0:hd] / acc_s[hd:hd + 1]
    o_w = acc_w[0:hd] / acc_w[hd:hd + 1]
    o = gate(0) * oc_ref[...] + gate(1) * o_s + gate(2) * o_w
    for g in range(GROUP):
        o_ref[0, g * hd:(g + 1) * hd, :] = o[:, g * t:(g + 1) * t].astype(o_ref.dtype)


def _nsa_seq_t(qt, kn, vt, cmp_n, cmp_t, gt, rel_bias, *, t):
    b, nq, s = qt.shape
    hd = HEAD_DIM
    ns = s // SEL_BLK
    nc = s // CMP_BLK
    n_top = min(N_SEL, ns)
    assert ns == hd and s % t == 0 and t % LANE == 0 and WINDOW % t == 0
    n_near = min(-(-(MAX_DISTANCE + t - 1) // t), s // t)
    n_wt = WINDOW // t + 1
    assert n_wt <= n_near + 1
    bias_s = _bias_table(rel_bias, n_near, t, t, off=(n_near - 1) * t, ostride=-t, rs=-1, cs=1, grouped=True,
                         scale=LOG2E)
    bias_s = bias_s.reshape(N_KV_HEADS, n_near * t, GROUP * t)
    bias_w = _bias_table(rel_bias, n_wt, t, t, off=(n_wt - 1) * t, ostride=-t, rs=-1, cs=1, dmax=WINDOW,
                         grouped=True, scale=LOG2E)
    bias_w = bias_w.reshape(N_KV_HEADS, n_wt * t, GROUP * t)
    bias_c = _bias_table(rel_bias, s // t, nc, t, off=-(CMP_BLK - 1), ostride=t, rs=-CMP_BLK, cs=1, grouped=True,
                         scale=LOG2E)
    far = rel_bias[N_BUCKETS - 1] * LOG2E
    gw = GROUP * hd
    gt_ = GROUP * t
    return pl.pallas_call(
        functools.partial(_nsa_seq_t_kernel, t=t, n_near=n_near, n_top=n_top),
        grid=(b, N_KV_HEADS, s // t),
        in_specs=[pl.BlockSpec((1, gw, t), lambda bb, k, i: (bb, k, i)),
                  pl.BlockSpec((1, s, 2 * hd), lambda bb, k, i: (bb, 0, k)),
                  pl.BlockSpec((1, 2 * hd, s), lambda bb, k, i: (bb, k, 0)),
                  pl.BlockSpec((1, 1, nc, hd), lambda bb, k, i: (bb, k, 0, 0)),
                  pl.BlockSpec((1, 1, hd, nc), lambda bb, k, i: (bb, N_KV_HEADS + k, 0, 0)),
                  pl.BlockSpec((1, gt.shape[1], t), lambda bb, k, i: (bb, 0, i)),
                  pl.BlockSpec((1, 1, nc, gt_), lambda bb, k, i: (k, i, 0, 0)),
                  pl.BlockSpec((1, n_near * t, gt_), lambda bb, k, i: (k, 0, 0), pipeline_mode=pl.Buffered(1)),
                  pl.BlockSpec((1, n_wt * t, gt_), lambda bb, k, i: (k, 0, 0), pipeline_mode=pl.Buffered(1)),
                  pl.BlockSpec(memory_space=pltpu.SMEM)],
        out_specs=pl.BlockSpec((1, gw, t), lambda bb, k, i: (bb, k, i)),
        out_shape=jax.ShapeDtypeStruct((b, nq, s), BF16),
        scratch_shapes=[pltpu.VMEM((s, 2 * hd), BF16), pltpu.VMEM((s, 2 * hd), BF16),
                        pltpu.VMEM((V_ROWS, s), BF16), pltpu.VMEM((V_ROWS, s), BF16),
                        pltpu.VMEM((2 * hd, gt_), BF16), pltpu.VMEM((t // LANE, nc, LANE), F32),
                        pltpu.VMEM((hd, gt_), F32),
                        pltpu.VMEM((2, 1, gt_), F32), pltpu.VMEM((2, V_ROWS, gt_), F32)],
        compiler_params=_cparams(("arbitrary", "arbitrary", "arbitrary")),
        name="nsa_seq",
    )(qt, kn, vt, cmp_n, cmp_t, gt, bias_c, bias_s, bias_w, far)


def _compress_pages_t_kernel(pt_ref, *refs, pg):
    page_refs = refs[:pg]
    pe_ref, w1_ref, w2_ref, pool_ref, o_ref = refs[pg:]
    for slot in range(2):
        for hh in range(N_KV_HEADS):
            xt = jnp.concatenate([page_refs[n][0, 0, slot, hh] for n in range(pg)], axis=1)
            o_ref[0, slot, hh] = _compress_t(xt, pe_ref[slot], w1_ref[slot], w2_ref[slot], pool_ref[...])


def _compress_pages_t(cache_t, page_table, layer, pet, w1t, w2t, pool, *, pg):
    db, n_pages = page_table.shape
    bpp = PAGE_SIZE // CMP_BLK
    hd = HEAD_DIM

    def page_map(b, p, pt, n):
        return (pt[b * n_pages + p * pg + n], layer, 0, 0, 0, 0)

    full = lambda a: pl.BlockSpec(a.shape, lambda b, p, pt: (0,) * a.ndim)
    grid_spec = pltpu.PrefetchScalarGridSpec(
        num_scalar_prefetch=1,
        grid=(db, n_pages // pg),
        in_specs=[pl.BlockSpec((1, 1, 2, N_KV_HEADS, hd, PAGE_SIZE), functools.partial(page_map, n=n))
                  for n in range(pg)] + [full(pet), full(w1t), full(w2t), full(pool)],
        out_specs=pl.BlockSpec((1, 2, N_KV_HEADS, hd, pg * bpp), lambda b, p, pt: (b, 0, 0, 0, p)),
    )
    return pl.pallas_call(
        functools.partial(_compress_pages_t_kernel, pg=pg),
        grid_spec=grid_spec,
        out_shape=jax.ShapeDtypeStruct((db, 2, N_KV_HEADS, hd, n_pages * bpp), F32),
        compiler_params=_cparams(("arbitrary", "arbitrary")),
        name="compress_pages",
    )(page_table.reshape(-1), *([cache_t] * pg), pet, w1t, w2t, pool)


def _nsa_step_sel_t_kernel(idx_ref, pt_ref, *refs, n_top, nsp):
    blk_refs = refs[:n_top]
    (q_ref, win_ref, new_ref, newt_ref, oc_ref, g_ref, bsel_ref, bwin_ref, o_ref, ks_ref, vs_ref, bs_ref) = refs[n_top:]
    b = pl.program_id(0)
    k = pl.program_id(1)
    hd = HEAD_DIM
    bpp = PAGE_SIZE // SEL_BLK
    qg = q_ref[0, 0]
    new = new_ref[0, 0]
    newt = newt_ref[0, 0]
    lane = lax.broadcasted_iota(jnp.int32, (hd, PAGE_SIZE), 1)
    k_newblk = jnp.where(lane == 0, newt[:, 2:3], 0.0)
    v_newblk = jnp.where(lane == 0, newt[:, 3:4], 0.0)
    for n in range(n_top):
        idn = idx_ref[(b * N_KV_HEADS + k) * n_top + n]
        is_new = idn >= nsp
        half = jnp.where(is_new, 0, jnp.minimum(idn, nsp - 1) % bpp)
        ks_ref[:, n * PAGE_SIZE:(n + 1) * PAGE_SIZE] = jnp.where(is_new, k_newblk, blk_refs[n][0, 0, 0, 0]).astype(BF16)
        vs_ref[:, n * PAGE_SIZE:(n + 1) * PAGE_SIZE] = jnp.where(is_new, v_newblk, blk_refs[n][0, 0, 1, 0]).astype(BF16)
        for g in range(GROUP):
            brow = bsel_ref[g, 0, pl.ds(idn, 1), :]
            for hf in range(bpp):
                c0 = n * PAGE_SIZE + hf * SEL_BLK
                bs_ref[g:g + 1, c0:c0 + SEL_BLK] = jnp.where(half == hf, brow, NEG)
    bias = bs_ref[...]
    ls = _dot(qg, ks_ref[...]) + bias
    mask = bias > 0.5 * NEG
    m = jnp.max(ls, axis=-1, keepdims=True)
    e = jnp.where(mask, jnp.exp(ls - m), 0.0)
    p = e / jnp.maximum(jnp.sum(e, axis=-1, keepdims=True), 1e-30)
    o_s = _dot_nt(p.astype(BF16), vs_ref[...])
    wb = win_ref.shape[5]
    bw = bwin_ref[0]
    bias_p = bw[:, 0:wb]
    bias_n = bw[:, wb:wb + 1]
    kn = new[4:5].astype(BF16).astype(F32)
    vn = new[5:6].astype(BF16).astype(F32)
    lw = _dot(qg, win_ref[0, 0, 0, 0].astype(BF16)) + bias_p
    lwn = jnp.sum(qg.astype(F32) * kn, axis=-1, keepdims=True) + bias_n
    mask_p = bias_p > 0.5 * NEG
    mask_n = bias_n > 0.5 * NEG
    m = jnp.maximum(jnp.max(lw, axis=-1, keepdims=True), lwn)
    e_p = jnp.where(mask_p, jnp.exp(lw - m), 0.0)
    e_n = jnp.where(mask_n, jnp.exp(lwn - m), 0.0)
    den = jnp.maximum(jnp.sum(e_p, axis=-1, keepdims=True) + e_n, 1e-30)
    o_w = (_dot_nt((e_p / den).astype(BF16), win_ref[0, 0, 1, 0].astype(BF16))
           + (e_n / den).astype(BF16).astype(F32) * vn)
    gates = jnp.broadcast_to(jax.nn.sigmoid(g_ref[0]), (GROUP, g_ref.shape[2]))
    col = lax.broadcasted_iota(jnp.int32, gates.shape, 1)
    head = k * GROUP + lax.broadcasted_iota(jnp.int32, gates.shape, 0)
    gate = lambda r: jnp.sum(jnp.where(col == head * 3 + r, gates, 0.0), axis=-1, keepdims=True)
    o_ref[0, 0] = gate(0) * oc_ref[0, 0] + gate(1) * o_s + gate(2) * o_w


def _nsa_step_sel_t(idx, page_table, cache_t, layer, q, win_t, kv_new, kv_new_t, o_c, g_nsa, bias_sel, bias_win,
                    *, past_len):
    db, n_pages = page_table.shape
    n_top = idx.shape[-1]
    hd = HEAD_DIM
    nsp = past_len // SEL_BLK
    bpp = PAGE_SIZE // SEL_BLK
    wb = win_t.shape[5]

    def blk_map(b, k, ix, pt, n):
        jp = jnp.minimum(ix[(b * N_KV_HEADS + k) * n_top + n], nsp - 1)
        return (pt[b * n_pages + jp // bpp], layer, 1, k, 0, 0)

    grid_spec = pltpu.PrefetchScalarGridSpec(
        num_scalar_prefetch=2,
        grid=(db, N_KV_HEADS),
        in_specs=[pl.BlockSpec((1, 1, 2, 1, hd, PAGE_SIZE), functools.partial(blk_map, n=n)) for n in range(n_top)]
        + [pl.BlockSpec((1, 1, GROUP, hd), lambda b, k, ix, pt: (b, k, 0, 0)),
           pl.BlockSpec((1, 1, 2, 1, hd, wb), lambda b, k, ix, pt: (b, layer, 0, k, 0, 0)),
           pl.BlockSpec((1, 1, 6, hd), lambda b, k, ix, pt: (b, k, 0, 0)),
           pl.BlockSpec((1, 1, hd, 8), lambda b, k, ix, pt: (b, k, 0, 0)),
           pl.BlockSpec((1, 1, GROUP, hd), lambda b, k, ix, pt: (b, k, 0, 0)),
           pl.BlockSpec((1, 1, g_nsa.shape[-1]), lambda b, k, ix, pt: (b, 0, 0)),
           pl.BlockSpec((GROUP, 1) + bias_sel.shape[2:], lambda b, k, ix, pt: (k, 0, 0, 0)),
           pl.BlockSpec((1, GROUP, bias_win.shape[-1]), lambda b, k, ix, pt: (k, 0, 0))],
        out_specs=pl.BlockSpec((1, 1, GROUP, hd), lambda b, k, ix, pt: (b, k, 0, 0)),
        scratch_shapes=[pltpu.VMEM((hd, n_top * PAGE_SIZE), BF16), pltpu.VMEM((hd, n_top * PAGE_SIZE), BF16),
                        pltpu.VMEM((GROUP, n_top * PAGE_SIZE), F32)],
    )
    return pl.pallas_call(
        functools.partial(_nsa_step_sel_t_kernel, n_top=n_top, nsp=nsp),
        grid_spec=grid_spec,
        out_shape=jax.ShapeDtypeStruct((db, N_KV_HEADS, GROUP, hd), F32),
        compiler_params=_cparams(("arbitrary", "arbitrary")),
        name="nsa_step_sel",
    )(idx.reshape(-1), page_table.reshape(-1), *([cache_t] * n_top), q, win_t, kv_new, kv_new_t, o_c, g_nsa,
      bias_sel, bias_win)


def _pick(n, pref):
    for c in pref:
        if n % c == 0:
            return c
    return n


def kernel(x_prompt, x_sample, cache_kv, state_win_kv, state_conv, state_ffn, page_table, c_prompt, c_sample,
           w_ada, b_ada, g_norm1, g_norm2, w_in, conv_w, conv_b, conv_ln_g, conv_ln_b, w_conv_out,
           cmp_pe, cmp_w1, cmp_w2, w_nsa_out, w_out, w_up, ffn_conv_w, ffn_conv_b, w_down, rel_bias, g_final):
    depth = w_ada.shape[0]
    assert depth == 1
    layer = 0
    b, s, d = x_prompt.shape
    db = x_sample.shape[0]
    assert x_sample.shape[1] == 1
    hd = HEAD_DIM
    d_conv = conv_w.shape[2]
    d_ff = w_down.shape[1]
    nq = N_HEADS * hd
    nkv = 6 * N_KV_HEADS * hd
    past_len = page_table.shape[1] * PAGE_SIZE
    wb = state_win_kv.shape[4]

    o0, o1, o2, o3 = 2 * d_conv, 2 * d_conv + nq, 2 * d_conv + nq + nkv, 2 * d_conv + nq + nkv + 3 * N_HEADS
    wi = w_in[layer]
    wu = wi[:, :o0].astype(BF16)
    wq = wi[:, o0:o1].astype(BF16)
    wkv = wi[:, o1:o2].astype(BF16)
    wkv6 = wkv.reshape(d, 6, N_KV_HEADS, hd)
    wkn = jnp.stack([wkv6[:, 2], wkv6[:, 4]], axis=2).reshape(d, 2 * N_KV_HEADS * hd)
    wg = jnp.pad(wi[:, o2:o3], ((0, 0), (0, 128 - 3 * N_HEADS))).astype(BF16)
    wbr = wi[:, o3:].astype(BF16)
    wco = w_conv_out[layer].astype(BF16)
    wno = w_nsa_out[layer].astype(BF16)
    wo = w_out[layer].astype(BF16)
    wup = w_up[layer].astype(BF16)
    wdn = w_down[layer].astype(BF16)

    mod = _ada(jnp.concatenate([c_prompt, c_sample], axis=0), w_ada[layer], b_ada[layer])
    mod = mod.reshape(b + db, 6, d)
    mods_p = mod[:b].transpose(1, 0, 2).reshape(6, b, 1, d)
    mods_s = mod[b:].transpose(1, 0, 2).reshape(6, 1, db, d)

    ts = _pick(s, (256, 128))
    u_p, qt_p, kn_p, vt_p, kv4t_p, kvwt_p, gt_p, br_p = _proj_seq(
        x_prompt, mods_p, g_norm1[layer], wu, wq.T, wkv.T, wkn, wg.T, wbr, ts=ts)
    cact_p = _conv_seq(u_p, conv_w[layer], conv_b[layer], conv_ln_g[layer], conv_ln_b[layer], ts=ts)
    pe_t = cmp_pe[layer].transpose(0, 2, 1)
    w1t = cmp_w1[layer].transpose(0, 2, 1).astype(BF16)
    w2t = cmp_w2[layer].transpose(0, 2, 1).astype(BF16)
    cmpn_p, cmpt_p = _compress_seq_t(kv4t_p, jnp.tile(pe_t, (1, 1, s // CMP_BLK)), w1t, w2t, _pool_matrix(s))
    ot_p = _nsa_seq_t(qt_p, kn_p, vt_p, cmpn_p, cmpt_p, gt_p, rel_bias, t=256)
    x1_p, h2_p = _merge(cact_p, ot_p, br_p, x_prompt, mods_p, g_norm2[layer], wco, wno, wo,
                        ts=_pick(s, (512, 256)), nsa_t=True)
    y_prompt, zt_p = _ffn(h2_p, x1_p, mods_p, g_final, wup, ffn_conv_w[layer], ffn_conv_b[layer], wdn, None,
                              ts=_pick(s, (512, 256)), fc=1536)
    w_len = min(WINDOW, s)
    kv_prompt = jnp.swapaxes(kv4t_p.reshape(b, 1, 4, N_KV_HEADS, hd, s), -1, -2)
    win_prompt = jnp.swapaxes(kvwt_p.reshape(b, 1, 2, N_KV_HEADS, hd, s)[..., s - w_len:], -1, -2)
    conv_prompt = u_p[:, None, s - (CONV_K - 1):]
    ffn_prompt = zt_p[:, None, zt_p.shape[1] - (FFN_K - 1):]

    xs = x_sample.reshape(1, db, d)
    u_s, q_s, kv_s, g_s, br_s = _proj_step(xs, mods_s, g_norm1[layer], wu, wq, wkv, wg, wbr, ts=db)
    kv_s = kv_s.reshape(db, 6, N_KV_HEADS, hd)
    cact_s = _conv_step(state_conv[:, layer:layer + 1], u_s[0], conv_w[layer], conv_b[layer],
                        conv_ln_g[layer], conv_ln_b[layer])
    cache_t = jnp.swapaxes(cache_kv, -1, -2)
    win_t = jnp.swapaxes(state_win_kv, -1, -2)
    pg = LANE // (PAGE_SIZE // CMP_BLK)
    assert page_table.shape[1] % pg == 0
    rp = pg * PAGE_SIZE
    cmp_past = _compress_pages_t(cache_t, page_table, layer, jnp.tile(pe_t, (1, 1, rp // CMP_BLK)), w1t, w2t,
                                 _pool_matrix(rp), pg=pg)
    ncp = cmp_past.shape[4]
    nnew = 128
    bias_cp = _bias_table(rel_bias, 1, 8, ncp, off=past_len - (CMP_BLK - 1), rs=0, cs=-CMP_BLK)[:, 0, 0]
    bias_cn = _bias_table(rel_bias, 1, 8, nnew, off=past_len - (CMP_BLK - 1) - ncp * CMP_BLK, rs=0, cs=-CMP_BLK)[:, 0, 0]
    gb = _pick(db, (8,))
    oc_s, idx_t = _nsa_step_cmp(q_s.reshape(db, N_HEADS, hd), cmp_past, kv_s[:, :2].reshape(db, 2 * N_KV_HEADS, hd),
                                cmp_pe[layer], cmp_w1[layer], cmp_w2[layer], bias_cp, bias_cn, gb=gb,
                                past_len=past_len)
    n_top = idx_t.shape[1]
    idx = idx_t[:, :, :gb * N_KV_HEADS].reshape(db // gb, n_top, gb, N_KV_HEADS).transpose(0, 2, 3, 1).reshape(db, N_KV_HEADS, n_top)
    ns_tot = (past_len + SEL_BLK) // SEL_BLK
    ns_rows = -(-ns_tot // 8) * 8
    bias_sel = _bias_table(rel_bias, 1, ns_rows, SEL_BLK, off=past_len, rs=-SEL_BLK, cs=-1)
    wcols = -(-(wb + 1) // 128) * 128
    bias_win = _bias_table(rel_bias, 1, 8, wcols, off=wb, rs=0, cs=-1, dmax=WINDOW)[:, 0, 0]
    bias_win = bias_win.reshape(N_KV_HEADS, GROUP, wcols)
    kv_new = kv_s.transpose(0, 2, 1, 3)
    kv_new_t = jnp.pad(kv_s.transpose(0, 2, 3, 1), ((0, 0), (0, 0), (0, 0), (0, 2)))
    o_s = _nsa_step_sel_t(idx, page_table, cache_t, layer, q_s.reshape(db, N_KV_HEADS, GROUP, hd), win_t,
                          kv_new, kv_new_t, oc_s.reshape(db, N_KV_HEADS, GROUP, hd),
                          g_s.reshape(db, 1, -1), bias_sel, bias_win, past_len=past_len)
    o_s = o_s.reshape(1, db, nq).astype(BF16)
    x1_s, h2_s = _merge(cact_s.reshape(1, db, d_conv), o_s, br_s, xs, mods_s, g_norm2[layer], wco, wno, wo,
                        ts=db, nsa_t=False)
    hist_f = state_ffn[:, layer]
    y_s, z_s = _ffn(h2_s, x1_s, mods_s, g_final, wup, ffn_conv_w[layer], ffn_conv_b[layer], wdn,
                           (hist_f[:, 0], hist_f[:, 1]), ts=db, fc=512)
    y_sample = y_s.reshape(db, 1, d)
    kv_sample = kv_s[:, :4].reshape(db, 1, 4, N_KV_HEADS, 1, hd)
    win_sample = jnp.concatenate([state_win_kv[:, layer, :, :, 1:], kv_s[:, 4:, :, None, :]], axis=3)[:, None]
    conv_sample = jnp.concatenate([state_conv[:, layer, 1:], u_s[0][:, None, :]], axis=1)[:, None]
    z_s = z_s[0]
    ffn_sample = jnp.concatenate([hist_f[:, 1:], z_s[:, None, :]], axis=1)[:, None]
    return (y_prompt, y_sample, kv_prompt, kv_sample, win_prompt, win_sample, conv_prompt, conv_sample,
            ffn_prompt, ffn_sample)
```

```python
import functools
import math

import jax
import jax.numpy as jnp
from jax import lax
from jax.experimental import pallas as pl
from jax.experimental.pallas import tpu as pltpu

F32 = jnp.float32
BF16 = jnp.bfloat16

N_HEADS = 16
HEAD_DIM = 64
N_KV_HEADS = 4
GROUP = N_HEADS // N_KV_HEADS
CMP_BLK = 32
SEL_BLK = 64
N_SEL = 16
WINDOW = 512
N_BUCKETS = 32
MAX_DISTANCE = 1024
CONV_K = 31
FFN_K = 3
PAGE_SIZE = 128
EPS = 1e-6
LOG2E = math.log2(math.e)
FORCE_SCORE = 1e4
NEG = -1e30
REMOVED = -3e38
LANE = 128
SUBLANE = 8
CONV_HALO = 32
VMEM_LIMIT = 56 * 1024 * 1024


def _cparams(sem):
    return pltpu.CompilerParams(dimension_semantics=sem, vmem_limit_bytes=VMEM_LIMIT)


def _dot(a, b):
    return jnp.dot(a, b, preferred_element_type=F32)


def _dot_nt(a, b):
    return lax.dot_general(a, b, (((1,), (1,)), ((), ())), preferred_element_type=F32)


def _silu(x):
    return x * jax.nn.sigmoid(x)


def _ada_kernel(c_ref, w_ref, b_ref, o_ref):
    s = _silu(c_ref[...]).astype(BF16)
    o_ref[...] = _dot(s, w_ref[...].astype(BF16)) + b_ref[...]


def _ada(c, w, b):
    m, d = c.shape
    n = w.shape[1]
    tn = 1536 if n % 1536 == 0 else n
    return pl.pallas_call(
        _ada_kernel,
        grid=(n // tn,),
        in_specs=[pl.BlockSpec((m, d), lambda j: (0, 0)),
                  pl.BlockSpec((d, tn), lambda j: (0, j)),
                  pl.BlockSpec((1, tn), lambda j: (0, j))],
        out_specs=pl.BlockSpec((m, tn), lambda j: (0, j)),
        out_shape=jax.ShapeDtypeStruct((m, n), F32),
        compiler_params=_cparams(("arbitrary",)),
        name="ada_mod",
    )(c, w, b.reshape(1, n))


def _mod_spec(mods, i, ts):
    d = mods.shape[-1]
    if mods.shape[2] == 1:
        return pl.BlockSpec((1, 1, 1, d), lambda b, s, *_: (i, b, 0, 0))
    return pl.BlockSpec((1, 1, ts, d), lambda b, s, *_: (i, b, s, 0))


def _norm_mod(x_ref, sh_ref, sc_ref, g_ref):
    x = x_ref[0]
    r = lax.rsqrt(jnp.mean(x * x, axis=-1, keepdims=True) + EPS)
    h = (x * r) * g_ref[...]
    return (h * (1.0 + sc_ref[0, 0]) + sh_ref[0, 0]).astype(BF16)


def _glu(hb, wu_ref, u_ref, d_conv):
    cw = 256
    for c in range(d_conv // cw):
        a = _dot(hb, wu_ref[:, c * cw:(c + 1) * cw])
        g = _dot(hb, wu_ref[:, d_conv + c * cw:d_conv + (c + 1) * cw])
        u_ref[0, :, c * cw:(c + 1) * cw] = a * jax.nn.sigmoid(g)


def _proj_step_kernel(x_ref, sh_ref, sc_ref, g1_ref, wu_ref, wq_ref, wkv_ref, wg_ref, wbr_ref,
                      u_ref, q_ref, kv_ref, g_ref, br_ref, *, d_conv):
    hb = _norm_mod(x_ref, sh_ref, sc_ref, g1_ref)
    _glu(hb, wu_ref, u_ref, d_conv)
    q_ref[0] = (_dot(hb, wq_ref[...]) * (HEAD_DIM ** -0.5)).astype(BF16)
    g_ref[0] = _dot(hb, wg_ref[...])
    br_ref[0] = _dot(hb, wbr_ref[...])
    kv_ref[0] = _dot(hb, wkv_ref[...])


def _resident(shape):
    nd = len(shape)
    return pl.BlockSpec(shape, lambda *_: (0,) * nd, pipeline_mode=pl.Buffered(1))


def _proj_step(x, mods, g1, wu, wq, wkv, wg, wbr, *, ts):
    b, s, d = x.shape
    d_conv = wu.shape[1] // 2
    row = lambda n: pl.BlockSpec((1, ts, n), lambda bb, ss: (bb, ss, 0))
    widths = (d_conv, wq.shape[1], wkv.shape[1], wg.shape[1], wbr.shape[1])
    dtypes = (F32, BF16, F32, F32, F32)
    return pl.pallas_call(
        functools.partial(_proj_step_kernel, d_conv=d_conv),
        grid=(b, s // ts),
        in_specs=[row(d), _mod_spec(mods, 0, ts), _mod_spec(mods, 1, ts), _resident((1, d)),
                  _resident(wu.shape), _resident(wq.shape), _resident(wkv.shape),
                  _resident(wg.shape), _resident(wbr.shape)],
        out_specs=[row(n) for n in widths],
        out_shape=[jax.ShapeDtypeStruct((b, s, n), dt) for n, dt in zip(widths, dtypes)],
        compiler_params=_cparams(("arbitrary", "arbitrary")),
        name="in_proj_step",
    )(x, mods, mods, g1.reshape(1, d), wu, wq, wkv, wg, wbr)


def _proj_seq_kernel(x_ref, sh_ref, sc_ref, g1_ref, wu_ref, wqt_ref, wkvt_ref, wkn_ref, wgt_ref, wbr_ref,
                     u_ref, qt_ref, kn_ref, vt_ref, kv4t_ref, kvwt_ref, gt_ref, br_ref, *, d_conv):
    hb = _norm_mod(x_ref, sh_ref, sc_ref, g1_ref)
    _glu(hb, wu_ref, u_ref, d_conv)
    qt_ref[0] = (_dot_nt(wqt_ref[...], hb) * (HEAD_DIM ** -0.5 * LOG2E)).astype(BF16)
    gt_ref[0] = _dot_nt(wgt_ref[...], hb)
    br_ref[0] = _dot(hb, wbr_ref[...])
    kn_ref[0] = _dot(hb, wkn_ref[...]).astype(BF16)
    kvt = _dot_nt(wkvt_ref[...], hb)
    hd = HEAD_DIM
    n4 = 4 * N_KV_HEADS * hd
    kv4t_ref[0] = kvt[:n4]
    kvwt_ref[0] = kvt[n4:]
    for hh in range(N_KV_HEADS):
        vsel = (3 * N_KV_HEADS + hh) * hd
        vwin = (5 * N_KV_HEADS + hh) * hd
        vt_ref[0, hh * 2 * hd:hh * 2 * hd + hd] = kvt[vsel:vsel + hd].astype(BF16)
        vt_ref[0, hh * 2 * hd + hd:(hh + 1) * 2 * hd] = kvt[vwin:vwin + hd].astype(BF16)


def _proj_seq(x, mods, g1, wu, wqt, wkvt, wkn, wgt, wbr, *, ts):
    b, s, d = x.shape
    d_conv = wu.shape[1] // 2
    hd = HEAD_DIM
    row = lambda n: pl.BlockSpec((1, ts, n), lambda bb, ss: (bb, ss, 0))
    col = lambda n: pl.BlockSpec((1, n, ts), lambda bb, ss: (bb, 0, ss))
    nq, ng = wqt.shape[0], wgt.shape[0]
    n4, nw, nv = 4 * N_KV_HEADS * hd, 2 * N_KV_HEADS * hd, 2 * N_KV_HEADS * hd
    w_len = min(WINDOW, s)
    assert w_len % ts == 0
    first_kept = (s - w_len) // ts
    win_spec = pl.BlockSpec((1, nw, ts), lambda bb, ss: (bb, 0, jnp.maximum(ss - first_kept, 0)))
    return pl.pallas_call(
        functools.partial(_proj_seq_kernel, d_conv=d_conv),
        grid=(b, s // ts),
        in_specs=[row(d), _mod_spec(mods, 0, ts), _mod_spec(mods, 1, ts), _resident((1, d)),
                  _resident(wu.shape), _resident(wqt.shape), _resident(wkvt.shape), _resident(wkn.shape),
                  _resident(wgt.shape), _resident(wbr.shape)],
        out_specs=[row(d_conv), col(nq), row(wkn.shape[1]), col(nv), col(n4), win_spec, col(ng), row(wbr.shape[1])],
        out_shape=[jax.ShapeDtypeStruct((b, s, d_conv), F32),
                   jax.ShapeDtypeStruct((b, nq, s), BF16),
                   jax.ShapeDtypeStruct((b, s, wkn.shape[1]), BF16),
                   jax.ShapeDtypeStruct((b, nv, s), BF16),
                   jax.ShapeDtypeStruct((b, n4, s), F32),
                   jax.ShapeDtypeStruct((b, nw, w_len), F32),
                   jax.ShapeDtypeStruct((b, ng, s), F32),
                   jax.ShapeDtypeStruct((b, s, wbr.shape[1]), F32)],
        compiler_params=_cparams(("arbitrary", "arbitrary")),
        name="in_proj_seq",
    )(x, mods, mods, g1.reshape(1, d), wu, wqt, wkvt, wkn, wgt, wbr)


def _ln_silu(y, g, b):
    mu = jnp.mean(y, axis=-1, keepdims=True)
    yc = y - mu
    var = jnp.mean(yc * yc, axis=-1, keepdims=True)
    return _silu(yc * lax.rsqrt(var + EPS) * g + b)


def _conv_seq_kernel(u_ref, halo_ref, w_ref, b_ref, lg_ref, lb_ref, o_ref, xs_ref, xsh_ref, acc_ref, *, ts, rb):
    i = pl.program_id(1)
    xs_ref[0:CONV_HALO] = jnp.where(i > 0, halo_ref[0], 0.0)
    xs_ref[CONV_HALO:CONV_HALO + ts] = u_ref[0]
    c = u_ref.shape[2]
    lane = LANE
    first = CONV_HALO - (CONV_K - 1)
    rows = CONV_HALO + ts - SUBLANE

    def col_body(ci, carry):
        c0 = pl.multiple_of(ci * lane, lane)
        for sh in range(1, SUBLANE):
            xsh_ref[sh, 0:rows, :] = xs_ref[pl.ds(sh, rows), pl.ds(c0, lane)]
        for r0 in range(0, ts, rb):
            acc = jnp.zeros((rb, lane), F32)
            for k in range(CONV_K):
                sh = (first + k) % SUBLANE
                base = r0 + first + k - sh
                if sh == 0:
                    x = xs_ref[pl.ds(base, rb), pl.ds(c0, lane)]
                else:
                    x = xsh_ref[sh, pl.ds(base, rb), :]
                acc = acc + w_ref[k:k + 1, pl.ds(c0, lane)] * x
            acc_ref[pl.ds(r0, rb), pl.ds(c0, lane)] = acc
        return carry

    lax.fori_loop(0, c // lane, col_body, 0)
    o_ref[0] = _ln_silu(acc_ref[...] + b_ref[...], lg_ref[...], lb_ref[...]).astype(o_ref.dtype)


def _conv_seq(u, w, b, lg, lb, *, ts):
    bsz, s, c = u.shape
    hb = ts // CONV_HALO
    vec = lambda: pl.BlockSpec((1, c), lambda bb, ss: (0, 0))
    return pl.pallas_call(
        functools.partial(_conv_seq_kernel, ts=ts, rb=32),
        grid=(bsz, s // ts),
        in_specs=[pl.BlockSpec((1, ts, c), lambda bb, ss: (bb, ss, 0)),
                  pl.BlockSpec((1, CONV_HALO, c), lambda bb, ss: (bb, jnp.maximum(ss * hb - 1, 0), 0)),
                  pl.BlockSpec((CONV_K, c), lambda bb, ss: (0, 0)), vec(), vec(), vec()],
        out_specs=pl.BlockSpec((1, ts, c), lambda bb, ss: (bb, ss, 0)),
        out_shape=jax.ShapeDtypeStruct((bsz, s, c), BF16),
        scratch_shapes=[pltpu.VMEM((CONV_HALO + ts, c), F32), pltpu.VMEM((SUBLANE, CONV_HALO + ts, LANE), F32),
                        pltpu.VMEM((ts, c), F32)],
        compiler_params=_cparams(("arbitrary", "arbitrary")),
        name="conv_seq",
    )(u, u, w, b.reshape(1, c), lg.reshape(1, c), lb.reshape(1, c))


def _conv_step_kernel(hist_ref, u_ref, w_ref, b_ref, lg_ref, lb_ref, o_ref):
    hist = hist_ref[:, 0]
    y = jnp.sum(hist * w_ref[0:CONV_K - 1][None], axis=1)
    y = y + u_ref[...] * w_ref[CONV_K - 1:CONV_K] + b_ref[...]
    o_ref[...] = _ln_silu(y, lg_ref[...], lb_ref[...]).astype(o_ref.dtype)


def _conv_step(hist, u, w, b, lg, lb):
    db, c = u.shape
    gb = 16 if db % 16 == 0 else db
    vec = lambda: pl.BlockSpec((1, c), lambda i: (0, 0))
    return pl.pallas_call(
        _conv_step_kernel,
        grid=(db // gb,),
        in_specs=[pl.BlockSpec((gb, 1, CONV_K - 1, c), lambda i: (i, 0, 0, 0)),
                  pl.BlockSpec((gb, c), lambda i: (i, 0)),
                  pl.BlockSpec((CONV_K, c), lambda i: (0, 0)), vec(), vec(), vec()],
        out_specs=pl.BlockSpec((gb, c), lambda i: (i, 0)),
        out_shape=jax.ShapeDtypeStruct((db, c), BF16),
        compiler_params=_cparams(("arbitrary",)),
        name="conv_step",
    )(hist, u, w, b.reshape(1, c), lg.reshape(1, c), lb.reshape(1, c))


def _merge_kernel(ca_ref, no_ref, br_ref, x_ref, m2_ref, sh_ref, sc_ref, g2_ref,
                  wco_ref, wno_ref, wo_ref, x1_ref, h2_ref, *, nsa_t):
    d = x_ref.shape[2]
    ya = _dot(ca_ref[0], wco_ref[...])
    if nsa_t:
        yb = lax.dot_general(no_ref[0], wno_ref[...], (((0,), (0,)), ((), ())), preferred_element_type=F32)
    else:
        yb = _dot(no_ref[0], wno_ref[...])
    br = br_ref[0]
    mix = jax.nn.sigmoid(br[:, :d]) * ya + jax.nn.sigmoid(br[:, d:]) * yb
    z = _dot(mix.astype(BF16), wo_ref[...])
    x1 = x_ref[0] + m2_ref[0, 0] * z
    x1_ref[0] = x1
    r = lax.rsqrt(jnp.mean(x1 * x1, axis=-1, keepdims=True) + EPS)
    h = (x1 * r) * g2_ref[...]
    h2_ref[0] = (h * (1.0 + sc_ref[0, 0]) + sh_ref[0, 0]).astype(BF16)


def _merge(cact, nsa_o, g_br, x, mods, g2, wco, wno, wo, *, ts, nsa_t):
    b, s, d = x.shape
    row = lambda n: pl.BlockSpec((1, ts, n), lambda bb, ss: (bb, ss, 0))
    nsa_spec = pl.BlockSpec((1, nsa_o.shape[1], ts), lambda bb, ss: (bb, 0, ss)) if nsa_t else row(nsa_o.shape[2])
    return pl.pallas_call(
        functools.partial(_merge_kernel, nsa_t=nsa_t),
        grid=(b, s // ts),
        in_specs=[row(cact.shape[2]), nsa_spec, row(2 * d), row(d),
                  _mod_spec(mods, 2, ts), _mod_spec(mods, 3, ts), _mod_spec(mods, 4, ts),
                  _resident((1, d)), _resident(wco.shape), _resident(wno.shape), _resident(wo.shape)],
        out_specs=[row(d), row(d)],
        out_shape=[jax.ShapeDtypeStruct((b, s, d), F32), jax.ShapeDtypeStruct((b, s, d), BF16)],
        compiler_params=_cparams(("arbitrary", "arbitrary")),
        name="merge",
    )(cact, nsa_o, g_br, x, mods, mods, mods, g2.reshape(1, d), wco, wno, wo)


def _gelu_tanh(x):
    return 0.5 * x * (1.0 + jnp.tanh(math.sqrt(2.0 / math.pi) * (x + 0.044715 * (x * x * x))))


def _ffn_kernel(*refs, seq_mode, ts, fc):
    if seq_mode:
        (h2_ref, x1_ref, m5_ref, gf_ref, wup_ref, cw_ref, cb_ref, wd_ref,
         y_ref, zt_ref, zs_ref, carry_ref) = refs
    else:
        (h2_ref, x1_ref, m5_ref, gf_ref, wup_ref, cw_ref, cb_ref, wd_ref, h0_ref, h1_ref,
         y_ref, zt_ref) = refs
    si = pl.program_id(1)
    d_ff = wd_ref.shape[0]
    h2 = h2_ref[0]
    acc = None
    for f in range(d_ff // fc):
        halves = []
        for part in range(2):
            c0 = part * d_ff + f * fc
            z = _dot(h2, wup_ref[:, c0:c0 + fc])
            w0, w1, w2 = (cw_ref[j:j + 1, c0:c0 + fc] for j in range(FFN_K))
            if seq_mode:
                slot = 2 * f + part
                buf = 2 * (f % 2) + part
                zs_ref[buf, 0:8] = jnp.where(si > 0, carry_ref[slot], 0.0)
                zs_ref[buf, 8:8 + ts] = z
                carry_ref[slot] = z[ts - 8:ts]
                zt_ref[0, :, c0:c0 + fc] = z[ts - 8:ts]
                zc = w0 * zs_ref[buf, pl.ds(6, ts)] + w1 * zs_ref[buf, pl.ds(7, ts)] + w2 * z
            else:
                zt_ref[0, :, c0:c0 + fc] = z
                zc = w0 * h0_ref[:, c0:c0 + fc] + w1 * h1_ref[:, c0:c0 + fc] + w2 * z
            halves.append(zc + cb_ref[:, c0:c0 + fc])
        act = (_gelu_tanh(halves[0]) * halves[1]).astype(BF16)
        contrib = _dot(act, wd_ref[f * fc:(f + 1) * fc, :])
        acc = contrib if acc is None else acc + contrib
    x2 = x1_ref[0] + m5_ref[0, 0] * acc
    r = lax.rsqrt(jnp.mean(x2 * x2, axis=-1, keepdims=True) + EPS)
    y_ref[0] = (x2 * r) * gf_ref[...]


def _ffn(h2, x1, mods, g_final, w_up, cw, cb, w_down, hist, *, ts, fc):
    b, s, d = x1.shape
    d_ff = w_down.shape[0]
    nf = d_ff // fc
    seq_mode = hist is None
    row = lambda n: pl.BlockSpec((1, ts, n), lambda bb, ss: (bb, ss, 0))
    in_specs = [row(d), row(d), _mod_spec(mods, 5, ts), _resident((1, d)),
                _resident(w_up.shape), _resident(cw.shape), _resident((1, 2 * d_ff)), _resident(w_down.shape)]
    args = [h2, x1, mods, g_final.reshape(1, d), w_up, cw, cb.reshape(1, -1), w_down]
    zrows = 8 * (s // ts) if seq_mode else ts
    scratch = []
    if seq_mode:
        scratch = [pltpu.VMEM((min(4, 2 * nf), ts + 8, fc), F32), pltpu.VMEM((2 * nf, 8, fc), F32)]
        zspec = pl.BlockSpec((1, 8, 2 * d_ff), lambda bb, ss: (bb, ss, 0))
    else:
        assert b == 1 and s == ts
        h0, h1 = hist
        in_specs += [pl.BlockSpec((ts, 2 * d_ff), lambda bb, ss: (0, 0))] * 2
        args += [h0, h1]
        zspec = pl.BlockSpec((1, ts, 2 * d_ff), lambda bb, ss: (bb, 0, 0))
    return pl.pallas_call(
        functools.partial(_ffn_kernel, seq_mode=seq_mode, ts=ts, fc=fc),
        grid=(b, s // ts),
        in_specs=in_specs,
        out_specs=[row(d), zspec],
        out_shape=[jax.ShapeDtypeStruct((b, s, d), F32),
                   jax.ShapeDtypeStruct((b, zrows, 2 * d_ff), F32)],
        scratch_shapes=scratch,
        compiler_params=_cparams(("arbitrary", "arbitrary")),
        name="ffn",
    )(*args)


def _t5_bucket(d):
    max_exact = N_BUCKETS // 2
    df = jnp.maximum(d, 1).astype(F32)
    large = max_exact + (jnp.log(df / max_exact) / math.log(MAX_DISTANCE / max_exact)
                         * (N_BUCKETS - max_exact)).astype(jnp.int32)
    large = jnp.minimum(large, N_BUCKETS - 1)
    return jnp.where(d < max_exact, d, large)


def _bias_kernel(rb_ref, o_ref, *, off, ostride, rs, cs, dmax, scale):
    hh = pl.program_id(0)
    o = pl.program_id(1)
    rows, cols = o_ref.shape[2], o_ref.shape[3]
    d = (off + o * ostride + rs * lax.broadcasted_iota(jnp.int32, (rows, cols), 0)
         + cs * lax.broadcasted_iota(jnp.int32, (rows, cols), 1))
    bucket = _t5_bucket(jnp.maximum(d, 0))
    val = jnp.zeros((rows, cols), F32)
    for bk in range(N_BUCKETS):
        val = jnp.where(bucket == bk, rb_ref[bk, hh], val)
    ok = d >= 0
    if dmax is not None:
        ok = ok & (d <= dmax)
    o_ref[0, 0] = jnp.where(ok, val * scale, NEG)


def _bias_table(rel_bias, n_o, rows, cols, *, off, ostride=0, rs, cs, dmax=None, grouped=False, scale=1.0):
    if grouped:
        out_spec = pl.BlockSpec((1, 1, rows, cols), lambda hh, o: (hh // GROUP, o, 0, hh % GROUP))
        out_shape = jax.ShapeDtypeStruct((N_KV_HEADS, n_o, rows, GROUP * cols), F32)
    else:
        out_spec = pl.BlockSpec((1, 1, rows, cols), lambda hh, o: (hh, o, 0, 0))
        out_shape = jax.ShapeDtypeStruct((N_HEADS, n_o, rows, cols), F32)
    return pl.pallas_call(
        functools.partial(_bias_kernel, off=off, ostride=ostride, rs=rs, cs=cs, dmax=dmax, scale=scale),
        grid=(N_HEADS, n_o),
        in_specs=[pl.BlockSpec(memory_space=pltpu.SMEM)],
        out_specs=out_spec,
        out_shape=out_shape,
        compiler_params=_cparams(("arbitrary", "arbitrary")),
        name="bias_table",
    )(rel_bias)


def _compress_rows(x, pe, w1b, w2b):
    r = x.shape[0]
    xb = (x.reshape(r // CMP_BLK, CMP_BLK, HEAD_DIM) + pe[None]).reshape(r, HEAD_DIM).astype(BF16)
    h = _silu(_dot(xb, w1b))
    hm = jnp.sum(h.reshape(r // CMP_BLK, CMP_BLK, HEAD_DIM), axis=1) * (1.0 / CMP_BLK)
    return _dot(hm.astype(BF16), w2b)


def _topk_axis0(score, n_top):
    ns = score.shape[0]
    js = lax.broadcasted_iota(jnp.int32, score.shape, 0)
    sel = jnp.zeros(score.shape, jnp.bool_)
    winners = []
    for _ in range(n_top):
        m = jnp.max(score, axis=0, keepdims=True)
        first = jnp.min(jnp.where(score == m, js, ns), axis=0, keepdims=True)
        hit = js == first
        sel = sel | hit
        score = jnp.where(hit, REMOVED, score)
        winners.append(first)
    return sel, winners


def _nsa_step_cmp_kernel(q_ref, past_ref, new_ref, pe_ref, w1_ref, w2_ref, bcp_ref, bcn_ref,
                         oc_ref, idx_ref, kn_ref, imp_ref, impt_ref, *, gb, past_len, n_top):
    hd = HEAD_DIM
    ncp = past_ref.shape[4]
    nnew = bcn_ref.shape[1]
    ratio = SEL_BLK // CMP_BLK
    new_pad = SEL_BLK
    rows = gb * N_KV_HEADS * new_pad
    cm_new = []
    for slot in range(2):
        r_in_blk = lax.broadcasted_iota(jnp.int32, (gb * N_KV_HEADS, new_pad, hd), 1)
        xnew = new_ref[:, slot * N_KV_HEADS:(slot + 1) * N_KV_HEADS, :].reshape(gb * N_KV_HEADS, 1, hd)
        x = jnp.where(r_in_blk == 0, xnew, 0.0).reshape(rows, hd)
        cm_new.append(_compress_rows(x, pe_ref[slot], w1_ref[slot].astype(BF16), w2_ref[slot].astype(BF16)))
    nb_new = new_pad // CMP_BLK
    kn_ref[...] = jnp.zeros(kn_ref.shape, F32)
    imp_ref[...] = jnp.zeros(imp_ref.shape, F32)
    for bi in range(gb):
        for k in range(N_KV_HEADS):
            r0 = (bi * N_KV_HEADS + k) * nb_new
            kn_ref[0, 0:nb_new] = cm_new[0][r0:r0 + nb_new]
            kn_ref[1, 0:nb_new] = cm_new[1][r0:r0 + nb_new]
            qg = q_ref[bi, k * GROUP:(k + 1) * GROUP, :]
            bias = jnp.concatenate([bcp_ref[k * GROUP:(k + 1) * GROUP, :],
                                    bcn_ref[k * GROUP:(k + 1) * GROUP, :]], axis=1)
            lc = jnp.concatenate([_dot(qg, past_ref[bi, 0, k].astype(BF16)),
                                  _dot_nt(qg, kn_ref[0].astype(BF16))], axis=1) + bias
            mask = bias > 0.5 * NEG
            m = jnp.max(lc, axis=-1, keepdims=True)
            e = jnp.where(mask, jnp.exp(lc - m), 0.0)
            p = e / jnp.maximum(jnp.sum(e, axis=-1, keepdims=True), 1e-30)
            pb = p.astype(BF16)
            oc_ref[bi, k * GROUP:(k + 1) * GROUP, :] = (_dot_nt(pb[:, :ncp], past_ref[bi, 1, k].astype(BF16))
                                                        + _dot(pb[:, ncp:], kn_ref[1].astype(BF16)))
            row = bi * N_KV_HEADS + k
            imp_ref[row:row + 1, :] = jnp.sum(p, axis=0, keepdims=True)
    impt_ref[...] = imp_ref[...].T
    nsr = (ncp + nnew) // ratio
    imps = impt_ref[pl.ds(0, nsr, stride=ratio), :]
    for rr in range(1, ratio):
        imps = imps + impt_ref[pl.ds(rr, nsr, stride=ratio), :]
    ns = (past_len + new_pad) // SEL_BLK
    js = lax.broadcasted_iota(jnp.int32, imps.shape, 0)
    qblk = past_len // SEL_BLK
    forced = (js == 0) | (js == qblk) | (js == qblk - 1)
    score = jnp.where(forced, FORCE_SCORE, jnp.where(js * SEL_BLK <= past_len, imps, -1.0))
    score = jnp.where(js < ns, score, REMOVED)
    _, winners = _topk_axis0(score, n_top)
    idx_ref[0] = jnp.concatenate(winners, axis=0)


def _nsa_step_cmp(q, cmp_past, kv_new01, pe, w1, w2, bias_cp, bias_cn, *, gb, past_len):
    db = q.shape[0]
    ncp = cmp_past.shape[4]
    nnew = bias_cn.shape[1]
    hd = HEAD_DIM
    ns = (past_len + SEL_BLK) // SEL_BLK
    n_top = min(N_SEL, ns)
    assert gb * N_KV_HEADS <= LANE
    full = lambda a: pl.BlockSpec(a.shape, lambda i: (0,) * a.ndim)
    return pl.pallas_call(
        functools.partial(_nsa_step_cmp_kernel, gb=gb, past_len=past_len, n_top=n_top),
        grid=(db // gb,),
        in_specs=[pl.BlockSpec((gb, N_HEADS, hd), lambda i: (i, 0, 0)),
                  pl.BlockSpec((gb, 2, N_KV_HEADS, hd, ncp), lambda i: (i, 0, 0, 0, 0)),
                  pl.BlockSpec((gb, 2 * N_KV_HEADS, hd), lambda i: (i, 0, 0)),
                  full(pe), full(w1), full(w2), full(bias_cp), full(bias_cn)],
        out_specs=[pl.BlockSpec((gb, N_HEADS, hd), lambda i: (i, 0, 0)),
                   pl.BlockSpec((1, n_top, LANE), lambda i: (i, 0, 0))],
        out_shape=[jax.ShapeDtypeStruct((db, N_HEADS, hd), F32),
                   jax.ShapeDtypeStruct((db // gb, n_top, LANE), jnp.int32)],
        scratch_shapes=[pltpu.VMEM((2, nnew, hd), F32), pltpu.VMEM((LANE, ncp + nnew), F32),
                        pltpu.VMEM((ncp + nnew, LANE), F32)],
        compiler_params=_cparams(("arbitrary",)),
        name="nsa_step_cmp",
    )(q, cmp_past, kv_new01, pe, w1, w2, bias_cp, bias_cn)


def _compress_t(xt, pet, w1t, w2t, pool):
    xb = (xt + pet).astype(BF16)
    h = _silu(_dot(w1t, xb))
    hm = _dot(h.astype(BF16), pool)
    return _dot(w2t, hm.astype(BF16))


def _pool_matrix(r):
    rows = lax.broadcasted_iota(jnp.int32, (r, r // CMP_BLK), 0) // CMP_BLK
    cols = lax.broadcasted_iota(jnp.int32, (r, r // CMP_BLK), 1)
    return jnp.where(rows == cols, 1.0 / CMP_BLK, 0.0).astype(BF16)


def _compress_seq_t_kernel(x_ref, pe_ref, w1_ref, w2_ref, pool_ref, on_ref, ot_ref):
    ct = _compress_t(x_ref[0], pe_ref[0], w1_ref[0], w2_ref[0], pool_ref[...])
    ot_ref[0, 0] = ct
    on_ref[0, 0] = ct.T


def _compress_seq_t(kv4t, pet, w1t, w2t, pool):
    b, _, s = kv4t.shape
    hd = HEAD_DIM
    n = 2 * N_KV_HEADS
    nc = s // CMP_BLK
    wspec = lambda: pl.BlockSpec((1, hd, hd), lambda bb, j: (j // N_KV_HEADS, 0, 0))
    return pl.pallas_call(
        _compress_seq_t_kernel,
        grid=(b, n),
        in_specs=[pl.BlockSpec((1, hd, s), lambda bb, j: (bb, j, 0)),
                  pl.BlockSpec((1, hd, s), lambda bb, j: (j // N_KV_HEADS, 0, 0)),
                  wspec(), wspec(), pl.BlockSpec(pool.shape, lambda bb, j: (0, 0))],
        out_specs=[pl.BlockSpec((1, 1, nc, hd), lambda bb, j: (bb, j, 0, 0)),
                   pl.BlockSpec((1, 1, hd, nc), lambda bb, j: (bb, j, 0, 0))],
        out_shape=[jax.ShapeDtypeStruct((b, n, nc, hd), F32), jax.ShapeDtypeStruct((b, n, hd, nc), F32)],
        compiler_params=_cparams(("arbitrary", "arbitrary")),
        name="compress_seq",
    )(kv4t, pet, w1t, w2t, pool)


FLASH_CHUNK = 256
V_ROWS = 80


def _flash_step_t(k_tile, qa, v_tile, bias, shift, m_ref, acc_ref, idx):
    rows = k_tile.shape[0]
    ck = FLASH_CHUNK
    ss = []
    for c in range(rows // ck):
        s = _dot(k_tile[c * ck:(c + 1) * ck], qa)
        if bias is not None:
            s = s + bias[c * ck:(c + 1) * ck]
        ss.append(s)
    m = m_ref[idx]
    acc = acc_ref[idx]
    for c, s in enumerate(ss):
        mx = jnp.max(s, axis=0, keepdims=True)
        if shift is not None:
            mx = mx + shift
        m_new = jnp.maximum(m, mx)
        alpha = jnp.exp2(m - m_new)
        ms = m_new if shift is None else m_new - shift
        p = jnp.exp2(s - ms).astype(BF16)
        acc = alpha * acc + _dot(v_tile[:, c * ck:(c + 1) * ck], p)
        m = m_new
    acc_ref[idx] = acc
    m_ref[idx] = m


def _nsa_seq_t_kernel(qt_ref, kn_ref, vt_ref, kc_ref, vct_ref, gt_ref, bc_ref, bs_ref, bw_ref, far_ref, o_ref,
                      kaug_ref, kwaug_ref, vs_ref, vw_ref, qa_ref, impt_ref, oc_ref, m_ref, acc_ref,
                      *, t, n_near, n_top):
    kh = pl.program_id(1)
    i = pl.program_id(2)
    s_len = kn_ref.shape[1]
    ns = s_len // SEL_BLK
    hd = HEAD_DIM

    @pl.when(i == 0)
    def _():
        blk = lax.broadcasted_iota(jnp.int32, (s_len, ns), 0) // SEL_BLK
        col = lax.broadcasted_iota(jnp.int32, (s_len, ns), 1)
        kaug_ref[:, 0:hd] = kn_ref[0, :, 0:hd]
        kaug_ref[:, hd:hd + ns] = jnp.where(blk == col, NEG, 0.0).astype(BF16)
        kwaug_ref[:, 0:hd] = kn_ref[0, :, hd:2 * hd]
        kwaug_ref[:, hd:hd + ns] = jnp.zeros((s_len, ns), BF16)
        row = lax.broadcasted_iota(jnp.int32, (V_ROWS - hd, s_len), 0)
        tail = jnp.where(row == 0, 1.0, 0.0).astype(BF16)
        vs_ref[0:hd] = vt_ref[0, 0:hd]
        vs_ref[hd:V_ROWS] = tail
        vw_ref[0:hd] = vt_ref[0, hd:2 * hd]
        vw_ref[hd:V_ROWS] = tail

    for g in range(GROUP):
        qa_ref[0:hd, g * t:(g + 1) * t] = qt_ref[0, g * hd:(g + 1) * hd, :]
    kc = kc_ref[0, 0].astype(BF16)
    vct = vct_ref[0, 0].astype(BF16)
    n_qt = s_len // t
    shift_rows = t // CMP_BLK
    bias = bc_ref[0, 0, pl.ds(pl.multiple_of((n_qt - 1 - i) * shift_rows, SUBLANE), 2 * ns), :]
    lc = _dot(kc, qa_ref[0:hd, :]) + bias
    mask = bias > 0.5 * NEG
    m = jnp.max(lc, axis=0, keepdims=True)
    e = jnp.where(mask, jnp.exp2(lc - m), 0.0)
    p = e / jnp.maximum(jnp.sum(e, axis=0, keepdims=True), 1e-30)
    oc_ref[...] = _dot(vct, p.astype(BF16))
    imp = p[:, 0:t]
    for g in range(1, GROUP):
        imp = imp + p[:, g * t:(g + 1) * t]
    ratio = SEL_BLK // CMP_BLK
    parts = []
    for c in range(t // LANE):
        impt_ref[c] = imp[:, c * LANE:(c + 1) * LANE]
        part = impt_ref[c, pl.ds(0, ns, stride=ratio), :]
        for rr in range(1, ratio):
            part = part + impt_ref[c, pl.ds(rr, ns, stride=ratio), :]
        parts.append(part)
    imps = jnp.concatenate(parts, axis=1)
    few = (i + 1) * t <= n_top * SEL_BLK

    @pl.when(few)
    def _():
        qa_ref[hd:hd + ns, :] = jnp.zeros((ns, GROUP * t), BF16)

    @pl.when(jnp.logical_not(few))
    def _():
        js = lax.broadcasted_iota(jnp.int32, (ns, t), 0)
        pos = i * t + lax.broadcasted_iota(jnp.int32, (ns, t), 1)
        qblk = pos // SEL_BLK
        forced = (js == 0) | (js == qblk) | (js == qblk - 1)
        score = jnp.where(forced, FORCE_SCORE, jnp.where(js * SEL_BLK <= pos, imps, -1.0))
        sel, _ = _topk_axis0(score, n_top)
        notsel = jnp.where(sel, 0.0, 1.0).astype(BF16)
        for g in range(GROUP):
            qa_ref[hd:hd + ns, g * t:(g + 1) * t] = notsel

    m_ref[...] = jnp.full(m_ref.shape, NEG, F32)
    acc_ref[...] = jnp.zeros(acc_ref.shape, F32)
    n_far = jnp.maximum(i - (n_near - 1), 0)
    far_shift = jnp.concatenate([jnp.full((1, t), far_ref[kh * GROUP + g], F32) for g in range(GROUP)], axis=1)

    def sel_step(r0, rows, bias, shift):
        _flash_step_t(kaug_ref[pl.ds(r0, rows), :], qa_ref[...], vs_ref[:, pl.ds(r0, rows)],
                      bias, shift, m_ref, acc_ref, 0)

    def tiled(step, bias_ref, n):
        def body(j, carry):
            r = pl.multiple_of((n - 1 - (i - j)) * t, t)
            step(pl.multiple_of(j * t, t), t, bias_ref[0, pl.ds(r, t), :])
            return carry
        lax.fori_loop(0, i + 1, body, 0)

    n_wt = WINDOW // t + 1

    def near_step(r0, rows, bias):
        sel_step(r0, rows, bias, None)

    def win_step(r0, rows, bias):
        _flash_step_t(kwaug_ref[pl.ds(r0, rows), :], qa_ref[...], vw_ref[:, pl.ds(r0, rows)],
                      bias, None, m_ref, acc_ref, 1)

    def far_body(jj, carry):
        sel_step(pl.multiple_of(jj * 8 * t, 8 * t), 8 * t, None, far_shift)
        return carry

    lax.fori_loop(0, n_far // 8, far_body, 0)
    for w in (4, 2, 1):
        @pl.when((n_far // w) % 2 == 1)
        def _(w=w):
            sel_step(pl.multiple_of((n_far // (2 * w)) * 2 * w * t, w * t), w * t, None, far_shift)

    @pl.when(i >= n_near - 1)
    def _():
        near_step(pl.multiple_of((i - (n_near - 1)) * t, t), n_near * t, bs_ref[0])

    @pl.when(i < n_near - 1)
    def _():
        tiled(near_step, bs_ref, n_near)

    @pl.when(i >= n_wt - 1)
    def _():
        win_step(pl.multiple_of((i - (n_wt - 1)) * t, t), n_wt * t, bw_ref[0])

    @pl.when(i < n_wt - 1)
    def _():
        tiled(win_step, bw_ref, n_wt)

    def gate(r):
        rows = [gt_ref[0, pl.ds((kh * GROUP + g) * 3 + r, 1), :] for g in range(GROUP)]
        return jax.nn.sigmoid(jnp.concatenate(rows, axis=1))

    acc_s = acc_ref[0]
    acc_w = acc_ref[1]
    o_s = acc_s[0:hd] / acc_s[hd:hd + 1]
    o_w = acc_w[0:hd] / acc_w[hd:hd + 1]
    o = gate(0) * oc_ref[...] + gate(1) * o_s + gate(2) * o_w
    for g in range(GROUP):
        o_ref[0, g * hd:(g + 1) * hd, :] = o[:, g * t:(g + 1) * t].astype(o_ref.dtype)


def _nsa_seq_t(qt, kn, vt, cmp_n, cmp_t, gt, rel_bias, *, t):
    b, nq, s = qt.shape
    hd = HEAD_DIM
    ns = s // SEL_BLK
    nc = s // CMP_BLK
    n_top = min(N_SEL, ns)
    assert ns == hd and s % t == 0 and t % LANE == 0 and WINDOW % t == 0
    n_near = min(-(-(MAX_DISTANCE + t - 1) // t), s // t)
    n_wt = WINDOW // t + 1
    assert n_wt <= n_near + 1
    bias_s = _bias_table(rel_bias, n_near, t, t, off=(n_near - 1) * t, ostride=-t, rs=-1, cs=1, grouped=True,
                         scale=LOG2E)
    bias_s = bias_s.reshape(N_KV_HEADS, n_near * t, GROUP * t)
    bias_w = _bias_table(rel_bias, n_wt, t, t, off=(n_wt - 1) * t, ostride=-t, rs=-1, cs=1, dmax=WINDOW,
                         grouped=True, scale=LOG2E)
    bias_w = bias_w.reshape(N_KV_HEADS, n_wt * t, GROUP * t)
    n_qt = s // t
    rows_c = nc + (n_qt - 1) * (t // CMP_BLK)
    assert t % (CMP_BLK * SUBLANE) == 0
    bias_c = _bias_table(rel_bias, 1, rows_c, t, off=(n_qt - 1) * t - (CMP_BLK - 1), rs=-CMP_BLK, cs=1,
                         grouped=True, scale=LOG2E)
    far = rel_bias[N_BUCKETS - 1] * LOG2E
    gw = GROUP * hd
    gt_ = GROUP * t
    return pl.pallas_call(
        functools.partial(_nsa_seq_t_kernel, t=t, n_near=n_near, n_top=n_top),
        grid=(b, N_KV_HEADS, s // t),
        in_specs=[pl.BlockSpec((1, gw, t), lambda bb, k, i: (bb, k, i)),
                  pl.BlockSpec((1, s, 2 * hd), lambda bb, k, i: (bb, 0, k)),
                  pl.BlockSpec((1, 2 * hd, s), lambda bb, k, i: (bb, k, 0)),
                  pl.BlockSpec((1, 1, nc, hd), lambda bb, k, i: (bb, k, 0, 0)),
                  pl.BlockSpec((1, 1, hd, nc), lambda bb, k, i: (bb, N_KV_HEADS + k, 0, 0)),
                  pl.BlockSpec((1, gt.shape[1], t), lambda bb, k, i: (bb, 0, i)),
                  pl.BlockSpec((1, 1, rows_c, gt_), lambda bb, k, i: (k, 0, 0, 0)),
                  pl.BlockSpec((1, n_near * t, gt_), lambda bb, k, i: (k, 0, 0), pipeline_mode=pl.Buffered(1)),
                  pl.BlockSpec((1, n_wt * t, gt_), lambda bb, k, i: (k, 0, 0), pipeline_mode=pl.Buffered(1)),
                  pl.BlockSpec(memory_space=pltpu.SMEM)],
        out_specs=pl.BlockSpec((1, gw, t), lambda bb, k, i: (bb, k, i)),
        out_shape=jax.ShapeDtypeStruct((b, nq, s), BF16),
        scratch_shapes=[pltpu.VMEM((s, 2 * hd), BF16), pltpu.VMEM((s, 2 * hd), BF16),
                        pltpu.VMEM((V_ROWS, s), BF16), pltpu.VMEM((V_ROWS, s), BF16),
                        pltpu.VMEM((2 * hd, gt_), BF16), pltpu.VMEM((t // LANE, nc, LANE), F32),
                        pltpu.VMEM((hd, gt_), F32),
                        pltpu.VMEM((2, 1, gt_), F32), pltpu.VMEM((2, V_ROWS, gt_), F32)],
        compiler_params=_cparams(("arbitrary", "arbitrary", "arbitrary")),
        name="nsa_seq",
    )(qt, kn, vt, cmp_n, cmp_t, gt, bias_c, bias_s, bias_w, far)


def _compress_pages_t_kernel(pt_ref, *refs, pg):
    page_refs = refs[:pg]
    pe_ref, w1_ref, w2_ref, pool_ref, o_ref = refs[pg:]
    for slot in range(2):
        for hh in range(N_KV_HEADS):
            xt = jnp.concatenate([page_refs[n][0, 0, slot, hh] for n in range(pg)], axis=1)
            o_ref[0, slot, hh] = _compress_t(xt, pe_ref[slot], w1_ref[slot], w2_ref[slot], pool_ref[...])


def _compress_pages_t(cache_t, page_table, layer, pet, w1t, w2t, pool, *, pg):
    db, n_pages = page_table.shape
    bpp = PAGE_SIZE // CMP_BLK
    hd = HEAD_DIM

    def page_map(b, p, pt, n):
        return (pt[b * n_pages + p * pg + n], layer, 0, 0, 0, 0)

    full = lambda a: pl.BlockSpec(a.shape, lambda b, p, pt: (0,) * a.ndim)
    grid_spec = pltpu.PrefetchScalarGridSpec(
        num_scalar_prefetch=1,
        grid=(db, n_pages // pg),
        in_specs=[pl.BlockSpec((1, 1, 2, N_KV_HEADS, hd, PAGE_SIZE), functools.partial(page_map, n=n))
                  for n in range(pg)] + [full(pet), full(w1t), full(w2t), full(pool)],
        out_specs=pl.BlockSpec((1, 2, N_KV_HEADS, hd, pg * bpp), lambda b, p, pt: (b, 0, 0, 0, p)),
    )
    return pl.pallas_call(
        functools.partial(_compress_pages_t_kernel, pg=pg),
        grid_spec=grid_spec,
        out_shape=jax.ShapeDtypeStruct((db, 2, N_KV_HEADS, hd, n_pages * bpp), F32),
        compiler_params=_cparams(("arbitrary", "arbitrary")),
        name="compress_pages",
    )(page_table.reshape(-1), *([cache_t] * pg), pet, w1t, w2t, pool)


def _nsa_step_sel_t_kernel(idx_ref, pt_ref, *refs, n_top, nsp):
    blk_refs = refs[:n_top]
    (q_ref, win_ref, new_ref, newt_ref, oc_ref, g_ref, bsel_ref, bwin_ref, o_ref, ks_ref, vs_ref, bs_ref) = refs[n_top:]
    b = pl.program_id(0)
    k = pl.program_id(1)
    hd = HEAD_DIM
    bpp = PAGE_SIZE // SEL_BLK
    qg = q_ref[0, 0]
    new = new_ref[0, 0]
    newt = newt_ref[0, 0]
    lane = lax.broadcasted_iota(jnp.int32, (hd, PAGE_SIZE), 1)
    k_newblk = jnp.where(lane == 0, newt[:, 2:3], 0.0)
    v_newblk = jnp.where(lane == 0, newt[:, 3:4], 0.0)
    for n in range(n_top):
        idn = idx_ref[(b * N_KV_HEADS + k) * n_top + n]
        is_new = idn >= nsp
        half = jnp.where(is_new, 0, jnp.minimum(idn, nsp - 1) % bpp)
        ks_ref[:, n * PAGE_SIZE:(n + 1) * PAGE_SIZE] = jnp.where(is_new, k_newblk, blk_refs[n][0, 0, 0, 0]).astype(BF16)
        vs_ref[:, n * PAGE_SIZE:(n + 1) * PAGE_SIZE] = jnp.where(is_new, v_newblk, blk_refs[n][0, 0, 1, 0]).astype(BF16)
        for g in range(GROUP):
            brow = bsel_ref[g, 0, pl.ds(idn, 1), :]
            for hf in range(bpp):
                c0 = n * PAGE_SIZE + hf * SEL_BLK
                bs_ref[g:g + 1, c0:c0 + SEL_BLK] = jnp.where(half == hf, brow, NEG)
    bias = bs_ref[...]
    ls = _dot(qg, ks_ref[...]) + bias
    mask = bias > 0.5 * NEG
    m = jnp.max(ls, axis=-1, keepdims=True)
    e = jnp.where(mask, jnp.exp(ls - m), 0.0)
    p = e / jnp.maximum(jnp.sum(e, axis=-1, keepdims=True), 1e-30)
    o_s = _dot_nt(p.astype(BF16), vs_ref[...])
    wb = win_ref.shape[5]
    bw = bwin_ref[0]
    bias_p = bw[:, 0:wb]
    bias_n = bw[:, wb:wb + 1]
    kn = new[4:5].astype(BF16).astype(F32)
    vn = new[5:6].astype(BF16).astype(F32)
    lw = _dot(qg, win_ref[0, 0, 0, 0].astype(BF16)) + bias_p
    lwn = jnp.sum(qg.astype(F32) * kn, axis=-1, keepdims=True) + bias_n
    mask_p = bias_p > 0.5 * NEG
    mask_n = bias_n > 0.5 * NEG
    m = jnp.maximum(jnp.max(lw, axis=-1, keepdims=True), lwn)
    e_p = jnp.where(mask_p, jnp.exp(lw - m), 0.0)
    e_n = jnp.where(mask_n, jnp.exp(lwn - m), 0.0)
    den = jnp.maximum(jnp.sum(e_p, axis=-1, keepdims=True) + e_n, 1e-30)
    o_w = (_dot_nt((e_p / den).astype(BF16), win_ref[0, 0, 1, 0].astype(BF16))
           + (e_n / den).astype(BF16).astype(F32) * vn)
    gates = jnp.broadcast_to(jax.nn.sigmoid(g_ref[0]), (GROUP, g_ref.shape[2]))
    col = lax.broadcasted_iota(jnp.int32, gates.shape, 1)
    head = k * GROUP + lax.broadcasted_iota(jnp.int32, gates.shape, 0)
    gate = lambda r: jnp.sum(jnp.where(col == head * 3 + r, gates, 0.0), axis=-1, keepdims=True)
    o_ref[0, 0] = gate(0) * oc_ref[0, 0] + gate(1) * o_s + gate(2) * o_w


def _nsa_step_sel_t(idx, page_table, cache_t, layer, q, win_t, kv_new, kv_new_t, o_c, g_nsa, bias_sel, bias_win,
                    *, past_len):
    db, n_pages = page_table.shape
    n_top = idx.shape[-1]
    hd = HEAD_DIM
    nsp = past_len // SEL_BLK
    bpp = PAGE_SIZE // SEL_BLK
    wb = win_t.shape[5]

    def blk_map(b, k, ix, pt, n):
        jp = jnp.minimum(ix[(b * N_KV_HEADS + k) * n_top + n], nsp - 1)
        return (pt[b * n_pages + jp // bpp], layer, 1, k, 0, 0)

    grid_spec = pltpu.PrefetchScalarGridSpec(
        num_scalar_prefetch=2,
        grid=(db, N_KV_HEADS),
        in_specs=[pl.BlockSpec((1, 1, 2, 1, hd, PAGE_SIZE), functools.partial(blk_map, n=n)) for n in range(n_top)]
        + [pl.BlockSpec((1, 1, GROUP, hd), lambda b, k, ix, pt: (b, k, 0, 0)),
           pl.BlockSpec((1, 1, 2, 1, hd, wb), lambda b, k, ix, pt: (b, layer, 0, k, 0, 0)),
           pl.BlockSpec((1, 1, 6, hd), lambda b, k, ix, pt: (b, k, 0, 0)),
           pl.BlockSpec((1, 1, hd, 8), lambda b, k, ix, pt: (b, k, 0, 0)),
           pl.BlockSpec((1, 1, GROUP, hd), lambda b, k, ix, pt: (b, k, 0, 0)),
           pl.BlockSpec((1, 1, g_nsa.shape[-1]), lambda b, k, ix, pt: (b, 0, 0)),
           pl.BlockSpec((GROUP, 1) + bias_sel.shape[2:], lambda b, k, ix, pt: (k, 0, 0, 0)),
           pl.BlockSpec((1, GROUP, bias_win.shape[-1]), lambda b, k, ix, pt: (k, 0, 0))],
        out_specs=pl.BlockSpec((1, 1, GROUP, hd), lambda b, k, ix, pt: (b, k, 0, 0)),
        scratch_shapes=[pltpu.VMEM((hd, n_top * PAGE_SIZE), BF16), pltpu.VMEM((hd, n_top * PAGE_SIZE), BF16),
                        pltpu.VMEM((GROUP, n_top * PAGE_SIZE), F32)],
    )
    return pl.pallas_call(
        functools.partial(_nsa_step_sel_t_kernel, n_top=n_top, nsp=nsp),
        grid_spec=grid_spec,
        out_shape=jax.ShapeDtypeStruct((db, N_KV_HEADS, GROUP, hd), F32),
        compiler_params=_cparams(("arbitrary", "arbitrary")),
        name="nsa_step_sel",
    )(idx.reshape(-1), page_table.reshape(-1), *([cache_t] * n_top), q, win_t, kv_new, kv_new_t, o_c, g_nsa,
      bias_sel, bias_win)


def _pick(n, pref):
    for c in pref:
        if n % c == 0:
            return c
    return n


def kernel(x_prompt, x_sample, cache_kv, state_win_kv, state_conv, state_ffn, page_table, c_prompt, c_sample,
           w_ada, b_ada, g_norm1, g_norm2, w_in, conv_w, conv_b, conv_ln_g, conv_ln_b, w_conv_out,
           cmp_pe, cmp_w1, cmp_w2, w_nsa_out, w_out, w_up, ffn_conv_w, ffn_conv_b, w_down, rel_bias, g_final):
    depth = w_ada.shape[0]
    assert depth == 1
    layer = 0
    b, s, d = x_prompt.shape
    db = x_sample.shape[0]
    assert x_sample.shape[1] == 1
    hd = HEAD_DIM
    d_conv = conv_w.shape[2]
    d_ff = w_down.shape[1]
    nq = N_HEADS * hd
    nkv = 6 * N_KV_HEADS * hd
    past_len = page_table.shape[1] * PAGE_SIZE
    wb = state_win_kv.shape[4]

    o0, o1, o2, o3 = 2 * d_conv, 2 * d_conv + nq, 2 * d_conv + nq + nkv, 2 * d_conv + nq + nkv + 3 * N_HEADS
    wi = w_in[layer]
    wu = wi[:, :o0].astype(BF16)
    wq = wi[:, o0:o1].astype(BF16)
    wkv = wi[:, o1:o2].astype(BF16)
    wkv6 = wkv.reshape(d, 6, N_KV_HEADS, hd)
    wkn = jnp.stack([wkv6[:, 2], wkv6[:, 4]], axis=2).reshape(d, 2 * N_KV_HEADS * hd)
    wg = jnp.pad(wi[:, o2:o3], ((0, 0), (0, 128 - 3 * N_HEADS))).astype(BF16)
    wbr = wi[:, o3:].astype(BF16)
    wco = w_conv_out[layer].astype(BF16)
    wno = w_nsa_out[layer].astype(BF16)
    wo = w_out[layer].astype(BF16)
    wup = w_up[layer].astype(BF16)
    wdn = w_down[layer].astype(BF16)

    mod = _ada(jnp.concatenate([c_prompt, c_sample], axis=0), w_ada[layer], b_ada[layer])
    mod = mod.reshape(b + db, 6, d)
    mods_p = mod[:b].transpose(1, 0, 2).reshape(6, b, 1, d)
    mods_s = mod[b:].transpose(1, 0, 2).reshape(6, 1, db, d)

    ts = _pick(s, (256, 128))
    u_p, qt_p, kn_p, vt_p, kv4t_p, kvwt_p, gt_p, br_p = _proj_seq(
        x_prompt, mods_p, g_norm1[layer], wu, wq.T, wkv.T, wkn, wg.T, wbr, ts=ts)
    cact_p = _conv_seq(u_p, conv_w[layer], conv_b[layer], conv_ln_g[layer], conv_ln_b[layer], ts=ts)
    pe_t = cmp_pe[layer].transpose(0, 2, 1)
    w1t = cmp_w1[layer].transpose(0, 2, 1).astype(BF16)
    w2t = cmp_w2[layer].transpose(0, 2, 1).astype(BF16)
    cmpn_p, cmpt_p = _compress_seq_t(kv4t_p, jnp.tile(pe_t, (1, 1, s // CMP_BLK)), w1t, w2t, _pool_matrix(s))
    ot_p = _nsa_seq_t(qt_p, kn_p, vt_p, cmpn_p, cmpt_p, gt_p, rel_bias, t=256)
    x1_p, h2_p = _merge(cact_p, ot_p, br_p, x_prompt, mods_p, g_norm2[layer], wco, wno, wo,
                        ts=_pick(s, (512, 256)), nsa_t=True)
    y_prompt, zt_p = _ffn(h2_p, x1_p, mods_p, g_final, wup, ffn_conv_w[layer], ffn_conv_b[layer], wdn, None,
                              ts=_pick(s, (512, 256)), fc=1536)
    w_len = min(WINDOW, s)
    kv_prompt = jnp.swapaxes(kv4t_p.reshape(b, 1, 4, N_KV_HEADS, hd, s), -1, -2)
    win_prompt = jnp.swapaxes(kvwt_p.reshape(b, 1, 2, N_KV_HEADS, hd, w_len), -1, -2)
    conv_prompt = u_p[:, None, s - (CONV_K - 1):]
    ffn_prompt = zt_p[:, None, zt_p.shape[1] - (FFN_K - 1):]

    xs = x_sample.reshape(1, db, d)
    u_s, q_s, kv_s, g_s, br_s = _proj_step(xs, mods_s, g_norm1[layer], wu, wq, wkv, wg, wbr, ts=db)
    kv_s = kv_s.reshape(db, 6, N_KV_HEADS, hd)
    cact_s = _conv_step(state_conv[:, layer:layer + 1], u_s[0], conv_w[layer], conv_b[layer],
                        conv_ln_g[layer], conv_ln_b[layer])
    cache_t = jnp.swapaxes(cache_kv, -1, -2)
    win_t = jnp.swapaxes(state_win_kv, -1, -2)
    pg = LANE // (PAGE_SIZE // CMP_BLK)
    assert page_table.shape[1] % pg == 0
    rp = pg * PAGE_SIZE
    cmp_past = _compress_pages_t(cache_t, page_table, layer, jnp.tile(pe_t, (1, 1, rp // CMP_BLK)), w1t, w2t,
                                 _pool_matrix(rp), pg=pg)
    ncp = cmp_past.shape[4]
    nnew = 128
    bias_cp = _bias_table(rel_bias, 1, 8, ncp, off=past_len - (CMP_BLK - 1), rs=0, cs=-CMP_BLK)[:, 0, 0]
    bias_cn = _bias_table(rel_bias, 1, 8, nnew, off=past_len - (CMP_BLK - 1) - ncp * CMP_BLK, rs=0, cs=-CMP_BLK)[:, 0, 0]
    gb = _pick(db, (8,))
    oc_s, idx_t = _nsa_step_cmp(q_s.reshape(db, N_HEADS, hd), cmp_past, kv_s[:, :2].reshape(db, 2 * N_KV_HEADS, hd),
                                cmp_pe[layer], cmp_w1[layer], cmp_w2[layer], bias_cp, bias_cn, gb=gb,
                                past_len=past_len)
    n_top = idx_t.shape[1]
    idx = idx_t[:, :, :gb * N_KV_HEADS].reshape(db // gb, n_top, gb, N_KV_HEADS).transpose(0, 2, 3, 1).reshape(db, N_KV_HEADS, n_top)
    ns_tot = (past_len + SEL_BLK) // SEL_BLK
    ns_rows = -(-ns_tot // 8) * 8
    bias_sel = _bias_table(rel_bias, 1, ns_rows, SEL_BLK, off=past_len, rs=-SEL_BLK, cs=-1)
    wcols = -(-(wb + 1) // 128) * 128
    bias_win = _bias_table(rel_bias, 1, 8, wcols, off=wb, rs=0, cs=-1, dmax=WINDOW)[:, 0, 0]
    bias_win = bias_win.reshape(N_KV_HEADS, GROUP, wcols)
    kv_new = kv_s.transpose(0, 2, 1, 3)
    kv_new_t = jnp.pad(kv_s.transpose(0, 2, 3, 1), ((0, 0), (0, 0), (0, 0), (0, 2)))
    o_s = _nsa_step_sel_t(idx, page_table, cache_t, layer, q_s.reshape(db, N_KV_HEADS, GROUP, hd), win_t,
                          kv_new, kv_new_t, oc_s.reshape(db, N_KV_HEADS, GROUP, hd),
                          g_s.reshape(db, 1, -1), bias_sel, bias_win, past_len=past_len)
    o_s = o_s.reshape(1, db, nq).astype(BF16)
    x1_s, h2_s = _merge(cact_s.reshape(1, db, d_conv), o_s, br_s, xs, mods_s, g_norm2[layer], wco, wno, wo,
                        ts=db, nsa_t=False)
    hist_f = state_ffn[:, layer]
    y_s, z_s = _ffn(h2_s, x1_s, mods_s, g_final, wup, ffn_conv_w[layer], ffn_conv_b[layer], wdn,
                           (hist_f[:, 0], hist_f[:, 1]), ts=db, fc=512)
    y_sample = y_s.reshape(db, 1, d)
    kv_sample = kv_s[:, :4].reshape(db, 1, 4, N_KV_HEADS, 1, hd)
    win_sample = jnp.concatenate([state_win_kv[:, layer, :, :, 1:], kv_s[:, 4:, :, None, :]], axis=3)[:, None]
    conv_sample = jnp.concatenate([state_conv[:, layer, 1:], u_s[0][:, None, :]], axis=1)[:, None]
    z_s = z_s[0]
    ffn_sample = jnp.concatenate([hist_f[:, 1:], z_s[:, None, :]], axis=1)[:, None]
    return (y_prompt, y_sample, kv_prompt, kv_sample, win_prompt, win_sample, conv_prompt, conv_sample,
            ffn_prompt, ffn_sample)
```

```python
import functools
import math

import jax
import jax.numpy as jnp
from jax import lax
from jax.experimental import pallas as pl
from jax.experimental.pallas import tpu as pltpu

F32 = jnp.float32
BF16 = jnp.bfloat16

N_HEADS = 16
HEAD_DIM = 64
N_KV_HEADS = 4
GROUP = N_HEADS // N_KV_HEADS
CMP_BLK = 32
SEL_BLK = 64
N_SEL = 16
WINDOW = 512
N_BUCKETS = 32
MAX_DISTANCE = 1024
CONV_K = 31
FFN_K = 3
PAGE_SIZE = 128
EPS = 1e-6
LOG2E = math.log2(math.e)
FORCE_SCORE = 1e4
NEG = -1e30
REMOVED = -3e38
LANE = 128
SUBLANE = 8
CONV_HALO = 32
VMEM_LIMIT = 56 * 1024 * 1024


def _cparams(sem):
    return pltpu.CompilerParams(dimension_semantics=sem, vmem_limit_bytes=VMEM_LIMIT)


def _dot(a, b):
    return jnp.dot(a, b, preferred_element_type=F32)


def _dot_nt(a, b):
    return lax.dot_general(a, b, (((1,), (1,)), ((), ())), preferred_element_type=F32)


def _silu(x):
    return x * (0.5 * jnp.tanh(0.5 * x) + 0.5)


def _ada_kernel(c_ref, w_ref, b_ref, o_ref):
    s = _silu(c_ref[...]).astype(BF16)
    o_ref[...] = _dot(s, w_ref[...].astype(BF16)) + b_ref[...]


def _ada(c, w, b):
    m, d = c.shape
    n = w.shape[1]
    tn = 1536 if n % 1536 == 0 else n
    return pl.pallas_call(
        _ada_kernel,
        grid=(n // tn,),
        in_specs=[pl.BlockSpec((m, d), lambda j: (0, 0)),
                  pl.BlockSpec((d, tn), lambda j: (0, j)),
                  pl.BlockSpec((1, tn), lambda j: (0, j))],
        out_specs=pl.BlockSpec((m, tn), lambda j: (0, j)),
        out_shape=jax.ShapeDtypeStruct((m, n), F32),
        compiler_params=_cparams(("arbitrary",)),
        name="ada_mod",
    )(c, w, b.reshape(1, n))


def _mod_spec(mods, i, ts):
    d = mods.shape[-1]
    if mods.shape[2] == 1:
        return pl.BlockSpec((1, 1, 1, d), lambda b, s, *_: (i, b, 0, 0))
    return pl.BlockSpec((1, 1, ts, d), lambda b, s, *_: (i, b, s, 0))


def _norm_mod(x_ref, sh_ref, sc_ref, g_ref):
    x = x_ref[0]
    r = lax.rsqrt(jnp.mean(x * x, axis=-1, keepdims=True) + EPS)
    h = (x * r) * g_ref[...]
    return (h * (1.0 + sc_ref[0, 0]) + sh_ref[0, 0]).astype(BF16)


def _glu(hb, wu_ref, u_ref, d_conv):
    cw = 256
    for c in range(d_conv // cw):
        a = _dot(hb, wu_ref[:, c * cw:(c + 1) * cw])
        g = _dot(hb, wu_ref[:, d_conv + c * cw:d_conv + (c + 1) * cw])
        u_ref[0, :, c * cw:(c + 1) * cw] = a * jax.nn.sigmoid(g)


def _proj_step_kernel(x_ref, sh_ref, sc_ref, g1_ref, wu_ref, wq_ref, wkv_ref, wg_ref, wbr_ref,
                      u_ref, q_ref, kv_ref, g_ref, br_ref, *, d_conv):
    hb = _norm_mod(x_ref, sh_ref, sc_ref, g1_ref)
    _glu(hb, wu_ref, u_ref, d_conv)
    q_ref[0] = (_dot(hb, wq_ref[...]) * (HEAD_DIM ** -0.5)).astype(BF16)
    g_ref[0] = _dot(hb, wg_ref[...])
    br_ref[0] = _dot(hb, wbr_ref[...])
    kv_ref[0] = _dot(hb, wkv_ref[...])


def _resident(shape):
    nd = len(shape)
    return pl.BlockSpec(shape, lambda *_: (0,) * nd, pipeline_mode=pl.Buffered(1))


def _proj_step(x, mods, g1, wu, wq, wkv, wg, wbr, *, ts):
    b, s, d = x.shape
    d_conv = wu.shape[1] // 2
    row = lambda n: pl.BlockSpec((1, ts, n), lambda bb, ss: (bb, ss, 0))
    widths = (d_conv, wq.shape[1], wkv.shape[1], wg.shape[1], wbr.shape[1])
    dtypes = (F32, BF16, F32, F32, F32)
    return pl.pallas_call(
        functools.partial(_proj_step_kernel, d_conv=d_conv),
        grid=(b, s // ts),
        in_specs=[row(d), _mod_spec(mods, 0, ts), _mod_spec(mods, 1, ts), _resident((1, d)),
                  _resident(wu.shape), _resident(wq.shape), _resident(wkv.shape),
                  _resident(wg.shape), _resident(wbr.shape)],
        out_specs=[row(n) for n in widths],
        out_shape=[jax.ShapeDtypeStruct((b, s, n), dt) for n, dt in zip(widths, dtypes)],
        compiler_params=_cparams(("arbitrary", "arbitrary")),
        name="in_proj_step",
    )(x, mods, mods, g1.reshape(1, d), wu, wq, wkv, wg, wbr)


def _proj_seq_kernel(x_ref, sh_ref, sc_ref, g1_ref, wu_ref, wqt_ref, wkvt_ref, wkn_ref, wgt_ref, wbr_ref,
                     u_ref, qt_ref, kn_ref, vt_ref, kv4t_ref, kvwt_ref, gt_ref, br_ref, *, d_conv):
    hb = _norm_mod(x_ref, sh_ref, sc_ref, g1_ref)
    _glu(hb, wu_ref, u_ref, d_conv)
    qt_ref[0] = (_dot_nt(wqt_ref[...], hb) * (HEAD_DIM ** -0.5 * LOG2E)).astype(BF16)
    gt_ref[0] = _dot_nt(wgt_ref[...], hb)
    br_ref[0] = _dot(hb, wbr_ref[...])
    kn_ref[0] = _dot(hb, wkn_ref[...]).astype(BF16)
    kvt = _dot_nt(wkvt_ref[...], hb)
    hd = HEAD_DIM
    n4 = 4 * N_KV_HEADS * hd
    kv4t_ref[0] = kvt[:n4]
    kvwt_ref[0] = kvt[n4:]
    for hh in range(N_KV_HEADS):
        vsel = (3 * N_KV_HEADS + hh) * hd
        vwin = (5 * N_KV_HEADS + hh) * hd
        vt_ref[0, hh * 2 * hd:hh * 2 * hd + hd] = kvt[vsel:vsel + hd].astype(BF16)
        vt_ref[0, hh * 2 * hd + hd:(hh + 1) * 2 * hd] = kvt[vwin:vwin + hd].astype(BF16)


def _proj_seq(x, mods, g1, wu, wqt, wkvt, wkn, wgt, wbr, *, ts):
    b, s, d = x.shape
    d_conv = wu.shape[1] // 2
    hd = HEAD_DIM
    row = lambda n: pl.BlockSpec((1, ts, n), lambda bb, ss: (bb, ss, 0))
    col = lambda n: pl.BlockSpec((1, n, ts), lambda bb, ss: (bb, 0, ss))
    nq, ng = wqt.shape[0], wgt.shape[0]
    n4, nw, nv = 4 * N_KV_HEADS * hd, 2 * N_KV_HEADS * hd, 2 * N_KV_HEADS * hd
    w_len = min(WINDOW, s)
    assert w_len % ts == 0
    first_kept = (s - w_len) // ts
    win_spec = pl.BlockSpec((1, nw, ts), lambda bb, ss: (bb, 0, jnp.maximum(ss - first_kept, 0)))
    return pl.pallas_call(
        functools.partial(_proj_seq_kernel, d_conv=d_conv),
        grid=(b, s // ts),
        in_specs=[row(d), _mod_spec(mods, 0, ts), _mod_spec(mods, 1, ts), _resident((1, d)),
                  _resident(wu.shape), _resident(wqt.shape), _resident(wkvt.shape), _resident(wkn.shape),
                  _resident(wgt.shape), _resident(wbr.shape)],
        out_specs=[row(d_conv), col(nq), row(wkn.shape[1]), col(nv), col(n4), win_spec, col(ng), row(wbr.shape[1])],
        out_shape=[jax.ShapeDtypeStruct((b, s, d_conv), F32),
                   jax.ShapeDtypeStruct((b, nq, s), BF16),
                   jax.ShapeDtypeStruct((b, s, wkn.shape[1]), BF16),
                   jax.ShapeDtypeStruct((b, nv, s), BF16),
                   jax.ShapeDtypeStruct((b, n4, s), F32),
                   jax.ShapeDtypeStruct((b, nw, w_len), F32),
                   jax.ShapeDtypeStruct((b, ng, s), F32),
                   jax.ShapeDtypeStruct((b, s, wbr.shape[1]), F32)],
        compiler_params=_cparams(("arbitrary", "arbitrary")),
        name="in_proj_seq",
    )(x, mods, mods, g1.reshape(1, d), wu, wqt, wkvt, wkn, wgt, wbr)


def _ln_silu(y, g, b):
    mu = jnp.mean(y, axis=-1, keepdims=True)
    yc = y - mu
    var = jnp.mean(yc * yc, axis=-1, keepdims=True)
    return _silu(yc * lax.rsqrt(var + EPS) * g + b)


def _conv_seq_kernel(u_ref, halo_ref, w_ref, b_ref, lg_ref, lb_ref, o_ref, xs_ref, xsh_ref, acc_ref, *, ts, rb):
    i = pl.program_id(1)
    xs_ref[0:CONV_HALO] = jnp.where(i > 0, halo_ref[0], 0.0)
    xs_ref[CONV_HALO:CONV_HALO + ts] = u_ref[0]
    c = u_ref.shape[2]
    lane = LANE
    first = CONV_HALO - (CONV_K - 1)
    rows = CONV_HALO + ts - SUBLANE

    def col_body(ci, carry):
        c0 = pl.multiple_of(ci * lane, lane)
        for sh in range(1, SUBLANE):
            xsh_ref[sh, 0:rows, :] = xs_ref[pl.ds(sh, rows), pl.ds(c0, lane)]
        for r0 in range(0, ts, rb):
            acc = jnp.zeros((rb, lane), F32)
            for k in range(CONV_K):
                sh = (first + k) % SUBLANE
                base = r0 + first + k - sh
                if sh == 0:
                    x = xs_ref[pl.ds(base, rb), pl.ds(c0, lane)]
                else:
                    x = xsh_ref[sh, pl.ds(base, rb), :]
                acc = acc + w_ref[k:k + 1, pl.ds(c0, lane)] * x
            acc_ref[pl.ds(r0, rb), pl.ds(c0, lane)] = acc
        return carry

    lax.fori_loop(0, c // lane, col_body, 0)
    o_ref[0] = _ln_silu(acc_ref[...] + b_ref[...], lg_ref[...], lb_ref[...]).astype(o_ref.dtype)


def _conv_seq(u, w, b, lg, lb, *, ts):
    bsz, s, c = u.shape
    hb = ts // CONV_HALO
    vec = lambda: pl.BlockSpec((1, c), lambda bb, ss: (0, 0))
    return pl.pallas_call(
        functools.partial(_conv_seq_kernel, ts=ts, rb=32),
        grid=(bsz, s // ts),
        in_specs=[pl.BlockSpec((1, ts, c), lambda bb, ss: (bb, ss, 0)),
                  pl.BlockSpec((1, CONV_HALO, c), lambda bb, ss: (bb, jnp.maximum(ss * hb - 1, 0), 0)),
                  pl.BlockSpec((CONV_K, c), lambda bb, ss: (0, 0)), vec(), vec(), vec()],
        out_specs=pl.BlockSpec((1, ts, c), lambda bb, ss: (bb, ss, 0)),
        out_shape=jax.ShapeDtypeStruct((bsz, s, c), BF16),
        scratch_shapes=[pltpu.VMEM((CONV_HALO + ts, c), F32), pltpu.VMEM((SUBLANE, CONV_HALO + ts, LANE), F32),
                        pltpu.VMEM((ts, c), F32)],
        compiler_params=_cparams(("arbitrary", "arbitrary")),
        name="conv_seq",
    )(u, u, w, b.reshape(1, c), lg.reshape(1, c), lb.reshape(1, c))


def _conv_step_kernel(hist_ref, u_ref, w_ref, b_ref, lg_ref, lb_ref, o_ref):
    hist = hist_ref[:, 0]
    y = jnp.sum(hist * w_ref[0:CONV_K - 1][None], axis=1)
    y = y + u_ref[...] * w_ref[CONV_K - 1:CONV_K] + b_ref[...]
    o_ref[...] = _ln_silu(y, lg_ref[...], lb_ref[...]).astype(o_ref.dtype)


def _conv_step(hist, u, w, b, lg, lb):
    db, c = u.shape
    gb = 16 if db % 16 == 0 else db
    vec = lambda: pl.BlockSpec((1, c), lambda i: (0, 0))
    return pl.pallas_call(
        _conv_step_kernel,
        grid=(db // gb,),
        in_specs=[pl.BlockSpec((gb, 1, CONV_K - 1, c), lambda i: (i, 0, 0, 0)),
                  pl.BlockSpec((gb, c), lambda i: (i, 0)),
                  pl.BlockSpec((CONV_K, c), lambda i: (0, 0)), vec(), vec(), vec()],
        out_specs=pl.BlockSpec((gb, c), lambda i: (i, 0)),
        out_shape=jax.ShapeDtypeStruct((db, c), BF16),
        compiler_params=_cparams(("arbitrary",)),
        name="conv_step",
    )(hist, u, w, b.reshape(1, c), lg.reshape(1, c), lb.reshape(1, c))


def _merge_kernel(ca_ref, no_ref, br_ref, x_ref, m2_ref, sh_ref, sc_ref, g2_ref,
                  wco_ref, wno_ref, wo_ref, x1_ref, h2_ref, *, nsa_t):
    d = x_ref.shape[2]
    ya = _dot(ca_ref[0], wco_ref[...])
    if nsa_t:
        yb = lax.dot_general(no_ref[0], wno_ref[...], (((0,), (0,)), ((), ())), preferred_element_type=F32)
    else:
        yb = _dot(no_ref[0], wno_ref[...])
    br = br_ref[0]
    mix = jax.nn.sigmoid(br[:, :d]) * ya + jax.nn.sigmoid(br[:, d:]) * yb
    z = _dot(mix.astype(BF16), wo_ref[...])
    x1 = x_ref[0] + m2_ref[0, 0] * z
    x1_ref[0] = x1
    r = lax.rsqrt(jnp.mean(x1 * x1, axis=-1, keepdims=True) + EPS)
    h = (x1 * r) * g2_ref[...]
    h2_ref[0] = (h * (1.0 + sc_ref[0, 0]) + sh_ref[0, 0]).astype(BF16)


def _merge(cact, nsa_o, g_br, x, mods, g2, wco, wno, wo, *, ts, nsa_t):
    b, s, d = x.shape
    row = lambda n: pl.BlockSpec((1, ts, n), lambda bb, ss: (bb, ss, 0))
    nsa_spec = pl.BlockSpec((1, nsa_o.shape[1], ts), lambda bb, ss: (bb, 0, ss)) if nsa_t else row(nsa_o.shape[2])
    return pl.pallas_call(
        functools.partial(_merge_kernel, nsa_t=nsa_t),
        grid=(b, s // ts),
        in_specs=[row(cact.shape[2]), nsa_spec, row(2 * d), row(d),
                  _mod_spec(mods, 2, ts), _mod_spec(mods, 3, ts), _mod_spec(mods, 4, ts),
                  _resident((1, d)), _resident(wco.shape), _resident(wno.shape), _resident(wo.shape)],
        out_specs=[row(d), row(d)],
        out_shape=[jax.ShapeDtypeStruct((b, s, d), F32), jax.ShapeDtypeStruct((b, s, d), BF16)],
        compiler_params=_cparams(("arbitrary", "arbitrary")),
        name="merge",
    )(cact, nsa_o, g_br, x, mods, mods, mods, g2.reshape(1, d), wco, wno, wo)


def _gelu_tanh(x):
    return 0.5 * x * (1.0 + jnp.tanh(math.sqrt(2.0 / math.pi) * (x + 0.044715 * (x * x * x))))


def _ffn_kernel(*refs, seq_mode, ts, fc):
    if seq_mode:
        (h2_ref, x1_ref, m5_ref, gf_ref, wup_ref, cw_ref, cb_ref, wd_ref,
         y_ref, zt_ref, zs_ref, carry_ref) = refs
    else:
        (h2_ref, x1_ref, m5_ref, gf_ref, wup_ref, cw_ref, cb_ref, wd_ref, h0_ref, h1_ref,
         y_ref, zt_ref) = refs
    si = pl.program_id(1)
    d_ff = wd_ref.shape[0]
    h2 = h2_ref[0]
    acc = None
    for f in range(d_ff // fc):
        halves = []
        for part in range(2):
            c0 = part * d_ff + f * fc
            z = _dot(h2, wup_ref[:, c0:c0 + fc])
            w0, w1, w2 = (cw_ref[j:j + 1, c0:c0 + fc] for j in range(FFN_K))
            if seq_mode:
                slot = 2 * f + part
                buf = 2 * (f % 2) + part
                zs_ref[buf, 0:8] = jnp.where(si > 0, carry_ref[slot], 0.0)
                zs_ref[buf, 8:8 + ts] = z
                carry_ref[slot] = z[ts - 8:ts]
                zt_ref[0, :, c0:c0 + fc] = z[ts - 8:ts]
                zc = w0 * zs_ref[buf, pl.ds(6, ts)] + w1 * zs_ref[buf, pl.ds(7, ts)] + w2 * z
            else:
                zt_ref[0, :, c0:c0 + fc] = z
                zc = w0 * h0_ref[:, c0:c0 + fc] + w1 * h1_ref[:, c0:c0 + fc] + w2 * z
            halves.append(zc + cb_ref[:, c0:c0 + fc])
        act = (_gelu_tanh(halves[0]) * halves[1]).astype(BF16)
        contrib = _dot(act, wd_ref[f * fc:(f + 1) * fc, :])
        acc = contrib if acc is None else acc + contrib
    x2 = x1_ref[0] + m5_ref[0, 0] * acc
    r = lax.rsqrt(jnp.mean(x2 * x2, axis=-1, keepdims=True) + EPS)
    y_ref[0] = (x2 * r) * gf_ref[...]


def _ffn(h2, x1, mods, g_final, w_up, cw, cb, w_down, hist, *, ts, fc):
    b, s, d = x1.shape
    d_ff = w_down.shape[0]
    nf = d_ff // fc
    seq_mode = hist is None
    row = lambda n: pl.BlockSpec((1, ts, n), lambda bb, ss: (bb, ss, 0))
    in_specs = [row(d), row(d), _mod_spec(mods, 5, ts), _resident((1, d)),
                _resident(w_up.shape), _resident(cw.shape), _resident((1, 2 * d_ff)), _resident(w_down.shape)]
    args = [h2, x1, mods, g_final.reshape(1, d), w_up, cw, cb.reshape(1, -1), w_down]
    zrows = 8 * (s // ts) if seq_mode else ts
    scratch = []
    if seq_mode:
        scratch = [pltpu.VMEM((min(4, 2 * nf), ts + 8, fc), F32), pltpu.VMEM((2 * nf, 8, fc), F32)]
        zspec = pl.BlockSpec((1, 8, 2 * d_ff), lambda bb, ss: (bb, ss, 0))
    else:
        assert b == 1 and s == ts
        h0, h1 = hist
        in_specs += [pl.BlockSpec((ts, 2 * d_ff), lambda bb, ss: (0, 0))] * 2
        args += [h0, h1]
        zspec = pl.BlockSpec((1, ts, 2 * d_ff), lambda bb, ss: (bb, 0, 0))
    return pl.pallas_call(
        functools.partial(_ffn_kernel, seq_mode=seq_mode, ts=ts, fc=fc),
        grid=(b, s // ts),
        in_specs=in_specs,
        out_specs=[row(d), zspec],
        out_shape=[jax.ShapeDtypeStruct((b, s, d), F32),
                   jax.ShapeDtypeStruct((b, zrows, 2 * d_ff), F32)],
        scratch_shapes=scratch,
        compiler_params=_cparams(("arbitrary", "arbitrary")),
        name="ffn",
    )(*args)


def _t5_bucket(d):
    max_exact = N_BUCKETS // 2
    df = jnp.maximum(d, 1).astype(F32)
    large = max_exact + (jnp.log(df / max_exact) / math.log(MAX_DISTANCE / max_exact)
                         * (N_BUCKETS - max_exact)).astype(jnp.int32)
    large = jnp.minimum(large, N_BUCKETS - 1)
    return jnp.where(d < max_exact, d, large)


def _bias_kernel(rb_ref, o_ref, *, off, ostride, rs, cs, dmax, scale):
    hh = pl.program_id(0)
    o = pl.program_id(1)
    rows, cols = o_ref.shape[2], o_ref.shape[3]
    d = (off + o * ostride + rs * lax.broadcasted_iota(jnp.int32, (rows, cols), 0)
         + cs * lax.broadcasted_iota(jnp.int32, (rows, cols), 1))
    bucket = _t5_bucket(jnp.maximum(d, 0))
    val = jnp.zeros((rows, cols), F32)
    for bk in range(N_BUCKETS):
        val = jnp.where(bucket == bk, rb_ref[bk, hh], val)
    ok = d >= 0
    if dmax is not None:
        ok = ok & (d <= dmax)
    o_ref[0, 0] = jnp.where(ok, val * scale, NEG)


def _bias_table(rel_bias, n_o, rows, cols, *, off, ostride=0, rs, cs, dmax=None, grouped=False, scale=1.0):
    if grouped:
        out_spec = pl.BlockSpec((1, 1, rows, cols), lambda hh, o: (hh // GROUP, o, 0, hh % GROUP))
        out_shape = jax.ShapeDtypeStruct((N_KV_HEADS, n_o, rows, GROUP * cols), F32)
    else:
        out_spec = pl.BlockSpec((1, 1, rows, cols), lambda hh, o: (hh, o, 0, 0))
        out_shape = jax.ShapeDtypeStruct((N_HEADS, n_o, rows, cols), F32)
    return pl.pallas_call(
        functools.partial(_bias_kernel, off=off, ostride=ostride, rs=rs, cs=cs, dmax=dmax, scale=scale),
        grid=(N_HEADS, n_o),
        in_specs=[pl.BlockSpec(memory_space=pltpu.SMEM)],
        out_specs=out_spec,
        out_shape=out_shape,
        compiler_params=_cparams(("arbitrary", "arbitrary")),
        name="bias_table",
    )(rel_bias)


def _compress_rows(x, pe, w1b, w2b):
    r = x.shape[0]
    xb = (x.reshape(r // CMP_BLK, CMP_BLK, HEAD_DIM) + pe[None]).reshape(r, HEAD_DIM).astype(BF16)
    h = _silu(_dot(xb, w1b))
    hm = jnp.sum(h.reshape(r // CMP_BLK, CMP_BLK, HEAD_DIM), axis=1) * (1.0 / CMP_BLK)
    return _dot(hm.astype(BF16), w2b)


def _topk_axis0(score, n_top):
    ns = score.shape[0]
    js = lax.broadcasted_iota(jnp.int32, score.shape, 0)
    sel = jnp.zeros(score.shape, jnp.bool_)
    winners = []
    for _ in range(n_top):
        m = jnp.max(score, axis=0, keepdims=True)
        first = jnp.min(jnp.where(score == m, js, ns), axis=0, keepdims=True)
        hit = js == first
        sel = sel | hit
        score = jnp.where(hit, REMOVED, score)
        winners.append(first)
    return sel, winners


def _nsa_step_cmp_kernel(q_ref, past_ref, new_ref, pe_ref, w1_ref, w2_ref, bcp_ref, bcn_ref,
                         oc_ref, idx_ref, kn_ref, imp_ref, impt_ref, *, gb, past_len, n_top):
    hd = HEAD_DIM
    ncp = past_ref.shape[4]
    nnew = bcn_ref.shape[1]
    ratio = SEL_BLK // CMP_BLK
    new_pad = SEL_BLK
    rows = gb * N_KV_HEADS * new_pad
    cm_new = []
    for slot in range(2):
        r_in_blk = lax.broadcasted_iota(jnp.int32, (gb * N_KV_HEADS, new_pad, hd), 1)
        xnew = new_ref[:, slot * N_KV_HEADS:(slot + 1) * N_KV_HEADS, :].reshape(gb * N_KV_HEADS, 1, hd)
        x = jnp.where(r_in_blk == 0, xnew, 0.0).reshape(rows, hd)
        cm_new.append(_compress_rows(x, pe_ref[slot], w1_ref[slot].astype(BF16), w2_ref[slot].astype(BF16)))
    nb_new = new_pad // CMP_BLK
    kn_ref[...] = jnp.zeros(kn_ref.shape, F32)
    imp_ref[...] = jnp.zeros(imp_ref.shape, F32)
    for bi in range(gb):
        for k in range(N_KV_HEADS):
            r0 = (bi * N_KV_HEADS + k) * nb_new
            kn_ref[0, 0:nb_new] = cm_new[0][r0:r0 + nb_new]
            kn_ref[1, 0:nb_new] = cm_new[1][r0:r0 + nb_new]
            qg = q_ref[bi, k * GROUP:(k + 1) * GROUP, :]
            bias = jnp.concatenate([bcp_ref[k * GROUP:(k + 1) * GROUP, :],
                                    bcn_ref[k * GROUP:(k + 1) * GROUP, :]], axis=1)
            lc = jnp.concatenate([_dot(qg, past_ref[bi, 0, k].astype(BF16)),
                                  _dot_nt(qg, kn_ref[0].astype(BF16))], axis=1) + bias
            mask = bias > 0.5 * NEG
            m = jnp.max(lc, axis=-1, keepdims=True)
            e = jnp.where(mask, jnp.exp(lc - m), 0.0)
            p = e / jnp.maximum(jnp.sum(e, axis=-1, keepdims=True), 1e-30)
            pb = p.astype(BF16)
            oc_ref[bi, k * GROUP:(k + 1) * GROUP, :] = (_dot_nt(pb[:, :ncp], past_ref[bi, 1, k].astype(BF16))
                                                        + _dot(pb[:, ncp:], kn_ref[1].astype(BF16)))
            row = bi * N_KV_HEADS + k
            imp_ref[row:row + 1, :] = jnp.sum(p, axis=0, keepdims=True)
    impt_ref[...] = imp_ref[...].T
    nsr = (ncp + nnew) // ratio
    imps = impt_ref[pl.ds(0, nsr, stride=ratio), :]
    for rr in range(1, ratio):
        imps = imps + impt_ref[pl.ds(rr, nsr, stride=ratio), :]
    ns = (past_len + new_pad) // SEL_BLK
    js = lax.broadcasted_iota(jnp.int32, imps.shape, 0)
    qblk = past_len // SEL_BLK
    forced = (js == 0) | (js == qblk) | (js == qblk - 1)
    score = jnp.where(forced, FORCE_SCORE, jnp.where(js * SEL_BLK <= past_len, imps, -1.0))
    score = jnp.where(js < ns, score, REMOVED)
    _, winners = _topk_axis0(score, n_top)
    idx_ref[0] = jnp.concatenate(winners, axis=0)


def _nsa_step_cmp(q, cmp_past, kv_new01, pe, w1, w2, bias_cp, bias_cn, *, gb, past_len):
    db = q.shape[0]
    ncp = cmp_past.shape[4]
    nnew = bias_cn.shape[1]
    hd = HEAD_DIM
    ns = (past_len + SEL_BLK) // SEL_BLK
    n_top = min(N_SEL, ns)
    assert gb * N_KV_HEADS <= LANE
    full = lambda a: pl.BlockSpec(a.shape, lambda i: (0,) * a.ndim)
    return pl.pallas_call(
        functools.partial(_nsa_step_cmp_kernel, gb=gb, past_len=past_len, n_top=n_top),
        grid=(db // gb,),
        in_specs=[pl.BlockSpec((gb, N_HEADS, hd), lambda i: (i, 0, 0)),
                  pl.BlockSpec((gb, 2, N_KV_HEADS, hd, ncp), lambda i: (i, 0, 0, 0, 0)),
                  pl.BlockSpec((gb, 2 * N_KV_HEADS, hd), lambda i: (i, 0, 0)),
                  full(pe), full(w1), full(w2), full(bias_cp), full(bias_cn)],
        out_specs=[pl.BlockSpec((gb, N_HEADS, hd), lambda i: (i, 0, 0)),
                   pl.BlockSpec((1, n_top, LANE), lambda i: (i, 0, 0))],
        out_shape=[jax.ShapeDtypeStruct((db, N_HEADS, hd), F32),
                   jax.ShapeDtypeStruct((db // gb, n_top, LANE), jnp.int32)],
        scratch_shapes=[pltpu.VMEM((2, nnew, hd), F32), pltpu.VMEM((LANE, ncp + nnew), F32),
                        pltpu.VMEM((ncp + nnew, LANE), F32)],
        compiler_params=_cparams(("arbitrary",)),
        name="nsa_step_cmp",
    )(q, cmp_past, kv_new01, pe, w1, w2, bias_cp, bias_cn)


def _compress_t(xt, pet, w1t, w2t, pool):
    xb = (xt + pet).astype(BF16)
    h = _silu(_dot(w1t, xb))
    hm = _dot(h.astype(BF16), pool)
    return _dot(w2t, hm.astype(BF16))


def _pool_matrix(r):
    rows = lax.broadcasted_iota(jnp.int32, (r, r // CMP_BLK), 0) // CMP_BLK
    cols = lax.broadcasted_iota(jnp.int32, (r, r // CMP_BLK), 1)
    return jnp.where(rows == cols, 1.0 / CMP_BLK, 0.0).astype(BF16)


def _compress_seq_t_kernel(x_ref, pe_ref, w1_ref, w2_ref, pool_ref, on_ref, ot_ref):
    ct = _compress_t(x_ref[0], pe_ref[0], w1_ref[0], w2_ref[0], pool_ref[...])
    ot_ref[0, 0] = ct
    on_ref[0, 0] = ct.T


def _compress_seq_t(kv4t, pet, w1t, w2t, pool):
    b, _, s = kv4t.shape
    hd = HEAD_DIM
    n = 2 * N_KV_HEADS
    nc = s // CMP_BLK
    wspec = lambda: pl.BlockSpec((1, hd, hd), lambda bb, j: (j // N_KV_HEADS, 0, 0))
    return pl.pallas_call(
        _compress_seq_t_kernel,
        grid=(b, n),
        in_specs=[pl.BlockSpec((1, hd, s), lambda bb, j: (bb, j, 0)),
                  pl.BlockSpec((1, hd, s), lambda bb, j: (j // N_KV_HEADS, 0, 0)),
                  wspec(), wspec(), pl.BlockSpec(pool.shape, lambda bb, j: (0, 0))],
        out_specs=[pl.BlockSpec((1, 1, nc, hd), lambda bb, j: (bb, j, 0, 0)),
                   pl.BlockSpec((1, 1, hd, nc), lambda bb, j: (bb, j, 0, 0))],
        out_shape=[jax.ShapeDtypeStruct((b, n, nc, hd), F32), jax.ShapeDtypeStruct((b, n, hd, nc), F32)],
        compiler_params=_cparams(("arbitrary", "arbitrary")),
        name="compress_seq",
    )(kv4t, pet, w1t, w2t, pool)


FLASH_CHUNK = 256
V_ROWS = 80


def _flash_step_t(k_tile, qa, v_tile, bias, shift, m_ref, acc_ref, idx):
    rows = k_tile.shape[0]
    ck = FLASH_CHUNK
    ss = []
    for c in range(rows // ck):
        s = _dot(k_tile[c * ck:(c + 1) * ck], qa)
        if bias is not None:
            s = s + bias[c * ck:(c + 1) * ck]
        ss.append(s)
    m = m_ref[idx]
    acc = acc_ref[idx]
    for c, s in enumerate(ss):
        mx = jnp.max(s, axis=0, keepdims=True)
        if shift is not None:
            mx = mx + shift
        m_new = jnp.maximum(m, mx)
        alpha = jnp.exp2(m - m_new)
        ms = m_new if shift is None else m_new - shift
        p = jnp.exp2(s - ms).astype(BF16)
        acc = alpha * acc + _dot(v_tile[:, c * ck:(c + 1) * ck], p)
        m = m_new
    acc_ref[idx] = acc
    m_ref[idx] = m


def _nsa_seq_t_kernel(qt_ref, kn_ref, vt_ref, kc_ref, vct_ref, gt_ref, bc_ref, bs_ref, bw_ref, far_ref, o_ref,
                      kaug_ref, kwaug_ref, vs_ref, vw_ref, qa_ref, impt_ref, oc_ref, m_ref, acc_ref,
                      *, t, n_near, n_top):
    kh = pl.program_id(1)
    i = pl.program_id(2)
    s_len = kn_ref.shape[1]
    ns = s_len // SEL_BLK
    hd = HEAD_DIM

    @pl.when(i == 0)
    def _():
        blk = lax.broadcasted_iota(jnp.int32, (s_len, ns), 0) // SEL_BLK
        col = lax.broadcasted_iota(jnp.int32, (s_len, ns), 1)
        kaug_ref[:, 0:hd] = kn_ref[0, :, 0:hd]
        kaug_ref[:, hd:hd + ns] = jnp.where(blk == col, NEG, 0.0).astype(BF16)
        kwaug_ref[:, 0:hd] = kn_ref[0, :, hd:2 * hd]
        kwaug_ref[:, hd:hd + ns] = jnp.zeros((s_len, ns), BF16)
        row = lax.broadcasted_iota(jnp.int32, (V_ROWS - hd, s_len), 0)
        tail = jnp.where(row == 0, 1.0, 0.0).astype(BF16)
        vs_ref[0:hd] = vt_ref[0, 0:hd]
        vs_ref[hd:V_ROWS] = tail
        vw_ref[0:hd] = vt_ref[0, hd:2 * hd]
        vw_ref[hd:V_ROWS] = tail

    for g in range(GROUP):
        qa_ref[0:hd, g * t:(g + 1) * t] = qt_ref[0, g * hd:(g + 1) * hd, :]
    kc = kc_ref[0, 0].astype(BF16)
    vct = vct_ref[0, 0].astype(BF16)
    n_qt = s_len // t
    shift_rows = t // CMP_BLK
    bias = bc_ref[0, 0, pl.ds(pl.multiple_of((n_qt - 1 - i) * shift_rows, SUBLANE), 2 * ns), :]
    lc = _dot(kc, qa_ref[0:hd, :]) + bias
    mask = bias > 0.5 * NEG
    m = jnp.max(lc, axis=0, keepdims=True)
    e = jnp.where(mask, jnp.exp2(lc - m), 0.0)
    p = e / jnp.maximum(jnp.sum(e, axis=0, keepdims=True), 1e-30)
    oc_ref[...] = _dot(vct, p.astype(BF16))
    imp = p[:, 0:t]
    for g in range(1, GROUP):
        imp = imp + p[:, g * t:(g + 1) * t]
    ratio = SEL_BLK // CMP_BLK
    parts = []
    for c in range(t // LANE):
        impt_ref[c] = imp[:, c * LANE:(c + 1) * LANE]
        part = impt_ref[c, pl.ds(0, ns, stride=ratio), :]
        for rr in range(1, ratio):
            part = part + impt_ref[c, pl.ds(rr, ns, stride=ratio), :]
        parts.append(part)
    imps = jnp.concatenate(parts, axis=1)
    few = (i + 1) * t <= n_top * SEL_BLK

    @pl.when(few)
    def _():
        qa_ref[hd:hd + ns, :] = jnp.zeros((ns, GROUP * t), BF16)

    @pl.when(jnp.logical_not(few))
    def _():
        js = lax.broadcasted_iota(jnp.int32, (ns, t), 0)
        pos = i * t + lax.broadcasted_iota(jnp.int32, (ns, t), 1)
        qblk = pos // SEL_BLK
        forced = (js == 0) | (js == qblk) | (js == qblk - 1)
        score = jnp.where(forced, FORCE_SCORE, jnp.where(js * SEL_BLK <= pos, imps, -1.0))
        sel, _ = _topk_axis0(score, n_top)
        notsel = jnp.where(sel, 0.0, 1.0).astype(BF16)
        for g in range(GROUP):
            qa_ref[hd:hd + ns, g * t:(g + 1) * t] = notsel

    m_ref[...] = jnp.full(m_ref.shape, NEG, F32)
    acc_ref[...] = jnp.zeros(acc_ref.shape, F32)
    n_far = jnp.maximum(i - (n_near - 1), 0)
    far_shift = jnp.concatenate([jnp.full((1, t), far_ref[kh * GROUP + g], F32) for g in range(GROUP)], axis=1)

    def sel_step(r0, rows, bias, shift):
        _flash_step_t(kaug_ref[pl.ds(r0, rows), :], qa_ref[...], vs_ref[:, pl.ds(r0, rows)],
                      bias, shift, m_ref, acc_ref, 0)

    def tiled(step, bias_ref, n):
        def body(j, carry):
            r = pl.multiple_of((n - 1 - (i - j)) * t, t)
            step(pl.multiple_of(j * t, t), t, bias_ref[0, pl.ds(r, t), :])
            return carry
        lax.fori_loop(0, i + 1, body, 0)

    n_wt = WINDOW // t + 1

    def near_step(r0, rows, bias):
        sel_step(r0, rows, bias, None)

    def win_step(r0, rows, bias):
        _flash_step_t(kwaug_ref[pl.ds(r0, rows), :], qa_ref[...], vw_ref[:, pl.ds(r0, rows)],
                      bias, None, m_ref, acc_ref, 1)

    def far_body(jj, carry):
        sel_step(pl.multiple_of(jj * 8 * t, 8 * t), 8 * t, None, far_shift)
        return carry

    lax.fori_loop(0, n_far // 8, far_body, 0)
    for w in (4, 2, 1):
        @pl.when((n_far // w) % 2 == 1)
        def _(w=w):
            sel_step(pl.multiple_of((n_far // (2 * w)) * 2 * w * t, w * t), w * t, None, far_shift)

    @pl.when(i >= n_near - 1)
    def _():
        near_step(pl.multiple_of((i - (n_near - 1)) * t, t), n_near * t, bs_ref[0])

    @pl.when(i < n_near - 1)
    def _():
        tiled(near_step, bs_ref, n_near)

    @pl.when(i >= n_wt - 1)
    def _():
        win_step(pl.multiple_of((i - (n_wt - 1)) * t, t), n_wt * t, bw_ref[0])

    @pl.when(i < n_wt - 1)
    def _():
        tiled(win_step, bw_ref, n_wt)

    def gate(r):
        rows = [gt_ref[0, pl.ds((kh * GROUP + g) * 3 + r, 1), :] for g in range(GROUP)]
        return jax.nn.sigmoid(jnp.concatenate(rows, axis=1))

    acc_s = acc_ref[0]
    acc_w = acc_ref[1]
    o_s = acc_s[0:hd] / acc_s[hd:hd + 1]
    o_w = acc_w[0:hd] / acc_w[hd:hd + 1]
    o = gate(0) * oc_ref[...] + gate(1) * o_s + gate(2) * o_w
    for g in range(GROUP):
        o_ref[0, g * hd:(g + 1) * hd, :] = o[:, g * t:(g + 1) * t].astype(o_ref.dtype)


def _nsa_seq_t(qt, kn, vt, cmp_n, cmp_t, gt, rel_bias, *, t):
    b, nq, s = qt.shape
    hd = HEAD_DIM
    ns = s // SEL_BLK
    nc = s // CMP_BLK
    n_top = min(N_SEL, ns)
    assert ns == hd and s % t == 0 and t % LANE == 0 and WINDOW % t == 0
    n_near = min(-(-(MAX_DISTANCE + t - 1) // t), s // t)
    n_wt = WINDOW // t + 1
    assert n_wt <= n_near + 1
    bias_s = _bias_table(rel_bias, n_near, t, t, off=(n_near - 1) * t, ostride=-t, rs=-1, cs=1, grouped=True,
                         scale=LOG2E)
    bias_s = bias_s.reshape(N_KV_HEADS, n_near * t, GROUP * t)
    bias_w = _bias_table(rel_bias, n_wt, t, t, off=(n_wt - 1) * t, ostride=-t, rs=-1, cs=1, dmax=WINDOW,
                         grouped=True, scale=LOG2E)
    bias_w = bias_w.reshape(N_KV_HEADS, n_wt * t, GROUP * t)
    n_qt = s // t
    rows_c = nc + (n_qt - 1) * (t // CMP_BLK)
    assert t % (CMP_BLK * SUBLANE) == 0
    bias_c = _bias_table(rel_bias, 1, rows_c, t, off=(n_qt - 1) * t - (CMP_BLK - 1), rs=-CMP_BLK, cs=1,
                         grouped=True, scale=LOG2E)
    far = rel_bias[N_BUCKETS - 1] * LOG2E
    gw = GROUP * hd
    gt_ = GROUP * t
    return pl.pallas_call(
        functools.partial(_nsa_seq_t_kernel, t=t, n_near=n_near, n_top=n_top),
        grid=(b, N_KV_HEADS, s // t),
        in_specs=[pl.BlockSpec((1, gw, t), lambda bb, k, i: (bb, k, i)),
                  pl.BlockSpec((1, s, 2 * hd), lambda bb, k, i: (bb, 0, k)),
                  pl.BlockSpec((1, 2 * hd, s), lambda bb, k, i: (bb, k, 0)),
                  pl.BlockSpec((1, 1, nc, hd), lambda bb, k, i: (bb, k, 0, 0)),
                  pl.BlockSpec((1, 1, hd, nc), lambda bb, k, i: (bb, N_KV_HEADS + k, 0, 0)),
                  pl.BlockSpec((1, gt.shape[1], t), lambda bb, k, i: (bb, 0, i)),
                  pl.BlockSpec((1, 1, rows_c, gt_), lambda bb, k, i: (k, 0, 0, 0)),
                  pl.BlockSpec((1, n_near * t, gt_), lambda bb, k, i: (k, 0, 0), pipeline_mode=pl.Buffered(1)),
                  pl.BlockSpec((1, n_wt * t, gt_), lambda bb, k, i: (k, 0, 0), pipeline_mode=pl.Buffered(1)),
                  pl.BlockSpec(memory_space=pltpu.SMEM)],
        out_specs=pl.BlockSpec((1, gw, t), lambda bb, k, i: (bb, k, i)),
        out_shape=jax.ShapeDtypeStruct((b, nq, s), BF16),
        scratch_shapes=[pltpu.VMEM((s, 2 * hd), BF16), pltpu.VMEM((s, 2 * hd), BF16),
                        pltpu.VMEM((V_ROWS, s), BF16), pltpu.VMEM((V_ROWS, s), BF16),
                        pltpu.VMEM((2 * hd, gt_), BF16), pltpu.VMEM((t // LANE, nc, LANE), F32),
                        pltpu.VMEM((hd, gt_), F32),
                        pltpu.VMEM((2, 1, gt_), F32), pltpu.VMEM((2, V_ROWS, gt_), F32)],
        compiler_params=_cparams(("arbitrary", "arbitrary", "arbitrary")),
        name="nsa_seq",
    )(qt, kn, vt, cmp_n, cmp_t, gt, bias_c, bias_s, bias_w, far)


def _compress_pages_t_kernel(pt_ref, *refs, pg):
    page_refs = refs[:pg]
    pe_ref, w1_ref, w2_ref, pool_ref, o_ref = refs[pg:]
    for slot in range(2):
        for hh in range(N_KV_HEADS):
            xt = jnp.concatenate([page_refs[n][0, 0, slot, hh] for n in range(pg)], axis=1)
            o_ref[0, slot, hh] = _compress_t(xt, pe_ref[slot], w1_ref[slot], w2_ref[slot], pool_ref[...])


def _compress_pages_t(cache_t, page_table, layer, pet, w1t, w2t, pool, *, pg):
    db, n_pages = page_table.shape
    bpp = PAGE_SIZE // CMP_BLK
    hd = HEAD_DIM

    def page_map(b, p, pt, n):
        return (pt[b * n_pages + p * pg + n], layer, 0, 0, 0, 0)

    full = lambda a: pl.BlockSpec(a.shape, lambda b, p, pt: (0,) * a.ndim)
    grid_spec = pltpu.PrefetchScalarGridSpec(
        num_scalar_prefetch=1,
        grid=(db, n_pages // pg),
        in_specs=[pl.BlockSpec((1, 1, 2, N_KV_HEADS, hd, PAGE_SIZE), functools.partial(page_map, n=n))
                  for n in range(pg)] + [full(pet), full(w1t), full(w2t), full(pool)],
        out_specs=pl.BlockSpec((1, 2, N_KV_HEADS, hd, pg * bpp), lambda b, p, pt: (b, 0, 0, 0, p)),
    )
    return pl.pallas_call(
        functools.partial(_compress_pages_t_kernel, pg=pg),
        grid_spec=grid_spec,
        out_shape=jax.ShapeDtypeStruct((db, 2, N_KV_HEADS, hd, n_pages * bpp), F32),
        compiler_params=_cparams(("arbitrary", "arbitrary")),
        name="compress_pages",
    )(page_table.reshape(-1), *([cache_t] * pg), pet, w1t, w2t, pool)


def _nsa_step_sel_t_kernel(idx_ref, pt_ref, *refs, n_top, nsp):
    blk_refs = refs[:n_top]
    (q_ref, win_ref, new_ref, newt_ref, oc_ref, g_ref, bsel_ref, bwin_ref, o_ref, ks_ref, vs_ref, bs_ref) = refs[n_top:]
    b = pl.program_id(0)
    k = pl.program_id(1)
    hd = HEAD_DIM
    bpp = PAGE_SIZE // SEL_BLK
    qg = q_ref[0, 0]
    new = new_ref[0, 0]
    newt = newt_ref[0, 0]
    lane = lax.broadcasted_iota(jnp.int32, (hd, PAGE_SIZE), 1)
    k_newblk = jnp.where(lane == 0, newt[:, 2:3], 0.0)
    v_newblk = jnp.where(lane == 0, newt[:, 3:4], 0.0)
    for n in range(n_top):
        idn = idx_ref[(b * N_KV_HEADS + k) * n_top + n]
        is_new = idn >= nsp
        half = jnp.where(is_new, 0, jnp.minimum(idn, nsp - 1) % bpp)
        ks_ref[:, n * PAGE_SIZE:(n + 1) * PAGE_SIZE] = jnp.where(is_new, k_newblk, blk_refs[n][0, 0, 0, 0]).astype(BF16)
        vs_ref[:, n * PAGE_SIZE:(n + 1) * PAGE_SIZE] = jnp.where(is_new, v_newblk, blk_refs[n][0, 0, 1, 0]).astype(BF16)
        for g in range(GROUP):
            brow = bsel_ref[g, 0, pl.ds(idn, 1), :]
            for hf in range(bpp):
                c0 = n * PAGE_SIZE + hf * SEL_BLK
                bs_ref[g:g + 1, c0:c0 + SEL_BLK] = jnp.where(half == hf, brow, NEG)
    bias = bs_ref[...]
    ls = _dot(qg, ks_ref[...]) + bias
    mask = bias > 0.5 * NEG
    m = jnp.max(ls, axis=-1, keepdims=True)
    e = jnp.where(mask, jnp.exp(ls - m), 0.0)
    p = e / jnp.maximum(jnp.sum(e, axis=-1, keepdims=True), 1e-30)
    o_s = _dot_nt(p.astype(BF16), vs_ref[...])
    wb = win_ref.shape[5]
    bw = bwin_ref[0]
    bias_p = bw[:, 0:wb]
    bias_n = bw[:, wb:wb + 1]
    kn = new[4:5].astype(BF16).astype(F32)
    vn = new[5:6].astype(BF16).astype(F32)
    lw = _dot(qg, win_ref[0, 0, 0, 0].astype(BF16)) + bias_p
    lwn = jnp.sum(qg.astype(F32) * kn, axis=-1, keepdims=True) + bias_n
    mask_p = bias_p > 0.5 * NEG
    mask_n = bias_n > 0.5 * NEG
    m = jnp.maximum(jnp.max(lw, axis=-1, keepdims=True), lwn)
    e_p = jnp.where(mask_p, jnp.exp(lw - m), 0.0)
    e_n = jnp.where(mask_n, jnp.exp(lwn - m), 0.0)
    den = jnp.maximum(jnp.sum(e_p, axis=-1, keepdims=True) + e_n, 1e-30)
    o_w = (_dot_nt((e_p / den).astype(BF16), win_ref[0, 0, 1, 0].astype(BF16))
           + (e_n / den).astype(BF16).astype(F32) * vn)
    gates = jnp.broadcast_to(jax.nn.sigmoid(g_ref[0]), (GROUP, g_ref.shape[2]))
    col = lax.broadcasted_iota(jnp.int32, gates.shape, 1)
    head = k * GROUP + lax.broadcasted_iota(jnp.int32, gates.shape, 0)
    gate = lambda r: jnp.sum(jnp.where(col == head * 3 + r, gates, 0.0), axis=-1, keepdims=True)
    o_ref[0, 0] = gate(0) * oc_ref[0, 0] + gate(1) * o_s + gate(2) * o_w


def _nsa_step_sel_t(idx, page_table, cache_t, layer, q, win_t, kv_new, kv_new_t, o_c, g_nsa, bias_sel, bias_win,
                    *, past_len):
    db, n_pages = page_table.shape
    n_top = idx.shape[-1]
    hd = HEAD_DIM
    nsp = past_len // SEL_BLK
    bpp = PAGE_SIZE // SEL_BLK
    wb = win_t.shape[5]

    def blk_map(b, k, ix, pt, n):
        jp = jnp.minimum(ix[(b * N_KV_HEADS + k) * n_top + n], nsp - 1)
        return (pt[b * n_pages + jp // bpp], layer, 1, k, 0, 0)

    grid_spec = pltpu.PrefetchScalarGridSpec(
        num_scalar_prefetch=2,
        grid=(db, N_KV_HEADS),
        in_specs=[pl.BlockSpec((1, 1, 2, 1, hd, PAGE_SIZE), functools.partial(blk_map, n=n)) for n in range(n_top)]
        + [pl.BlockSpec((1, 1, GROUP, hd), lambda b, k, ix, pt: (b, k, 0, 0)),
           pl.BlockSpec((1, 1, 2, 1, hd, wb), lambda b, k, ix, pt: (b, layer, 0, k, 0, 0)),
           pl.BlockSpec((1, 1, 6, hd), lambda b, k, ix, pt: (b, k, 0, 0)),
           pl.BlockSpec((1, 1, hd, 8), lambda b, k, ix, pt: (b, k, 0, 0)),
           pl.BlockSpec((1, 1, GROUP, hd), lambda b, k, ix, pt: (b, k, 0, 0)),
           pl.BlockSpec((1, 1, g_nsa.shape[-1]), lambda b, k, ix, pt: (b, 0, 0)),
           pl.BlockSpec((GROUP, 1) + bias_sel.shape[2:], lambda b, k, ix, pt: (k, 0, 0, 0)),
           pl.BlockSpec((1, GROUP, bias_win.shape[-1]), lambda b, k, ix, pt: (k, 0, 0))],
        out_specs=pl.BlockSpec((1, 1, GROUP, hd), lambda b, k, ix, pt: (b, k, 0, 0)),
        scratch_shapes=[pltpu.VMEM((hd, n_top * PAGE_SIZE), BF16), pltpu.VMEM((hd, n_top * PAGE_SIZE), BF16),
                        pltpu.VMEM((GROUP, n_top * PAGE_SIZE), F32)],
    )
    return pl.pallas_call(
        functools.partial(_nsa_step_sel_t_kernel, n_top=n_top, nsp=nsp),
        grid_spec=grid_spec,
        out_shape=jax.ShapeDtypeStruct((db, N_KV_HEADS, GROUP, hd), F32),
        compiler_params=_cparams(("arbitrary", "arbitrary")),
        name="nsa_step_sel",
    )(idx.reshape(-1), page_table.reshape(-1), *([cache_t] * n_top), q, win_t, kv_new, kv_new_t, o_c, g_nsa,
      bias_sel, bias_win)


def _pick(n, pref):
    for c in pref:
        if n % c == 0:
            return c
    return n


def kernel(x_prompt, x_sample, cache_kv, state_win_kv, state_conv, state_ffn, page_table, c_prompt, c_sample,
           w_ada, b_ada, g_norm1, g_norm2, w_in, conv_w, conv_b, conv_ln_g, conv_ln_b, w_conv_out,
           cmp_pe, cmp_w1, cmp_w2, w_nsa_out, w_out, w_up, ffn_conv_w, ffn_conv_b, w_down, rel_bias, g_final):
    depth = w_ada.shape[0]
    assert depth == 1
    layer = 0
    b, s, d = x_prompt.shape
    db = x_sample.shape[0]
    assert x_sample.shape[1] == 1
    hd = HEAD_DIM
    d_conv = conv_w.shape[2]
    d_ff = w_down.shape[1]
    nq = N_HEADS * hd
    nkv = 6 * N_KV_HEADS * hd
    past_len = page_table.shape[1] * PAGE_SIZE
    wb = state_win_kv.shape[4]

    o0, o1, o2, o3 = 2 * d_conv, 2 * d_conv + nq, 2 * d_conv + nq + nkv, 2 * d_conv + nq + nkv + 3 * N_HEADS
    wi = w_in[layer]
    wu = wi[:, :o0].astype(BF16)
    wq = wi[:, o0:o1].astype(BF16)
    wkv = wi[:, o1:o2].astype(BF16)
    wkv6 = wkv.reshape(d, 6, N_KV_HEADS, hd)
    wkn = jnp.stack([wkv6[:, 2], wkv6[:, 4]], axis=2).reshape(d, 2 * N_KV_HEADS * hd)
    wg = jnp.pad(wi[:, o2:o3], ((0, 0), (0, 128 - 3 * N_HEADS))).astype(BF16)
    wbr = wi[:, o3:].astype(BF16)
    wco = w_conv_out[layer].astype(BF16)
    wno = w_nsa_out[layer].astype(BF16)
    wo = w_out[layer].astype(BF16)
    wup = w_up[layer].astype(BF16)
    wdn = w_down[layer].astype(BF16)

    mod = _ada(jnp.concatenate([c_prompt, c_sample], axis=0), w_ada[layer], b_ada[layer])
    mod = mod.reshape(b + db, 6, d)
    mods_p = mod[:b].transpose(1, 0, 2).reshape(6, b, 1, d)
    mods_s = mod[b:].transpose(1, 0, 2).reshape(6, 1, db, d)

    ts = _pick(s, (256, 128))
    u_p, qt_p, kn_p, vt_p, kv4t_p, kvwt_p, gt_p, br_p = _proj_seq(
        x_prompt, mods_p, g_norm1[layer], wu, wq.T, wkv.T, wkn, wg.T, wbr, ts=ts)
    cact_p = _conv_seq(u_p, conv_w[layer], conv_b[layer], conv_ln_g[layer], conv_ln_b[layer], ts=ts)
    pe_t = cmp_pe[layer].transpose(0, 2, 1)
    w1t = cmp_w1[layer].transpose(0, 2, 1).astype(BF16)
    w2t = cmp_w2[layer].transpose(0, 2, 1).astype(BF16)
    cmpn_p, cmpt_p = _compress_seq_t(kv4t_p, jnp.tile(pe_t, (1, 1, s // CMP_BLK)), w1t, w2t, _pool_matrix(s))
    ot_p = _nsa_seq_t(qt_p, kn_p, vt_p, cmpn_p, cmpt_p, gt_p, rel_bias, t=256)
    x1_p, h2_p = _merge(cact_p, ot_p, br_p, x_prompt, mods_p, g_norm2[layer], wco, wno, wo,
                        ts=_pick(s, (512, 256)), nsa_t=True)
    y_prompt, zt_p = _ffn(h2_p, x1_p, mods_p, g_final, wup, ffn_conv_w[layer], ffn_conv_b[layer], wdn, None,
                              ts=_pick(s, (512, 256)), fc=1536)
    w_len = min(WINDOW, s)
    kv_prompt = jnp.swapaxes(kv4t_p.reshape(b, 1, 4, N_KV_HEADS, hd, s), -1, -2)
    win_prompt = jnp.swapaxes(kvwt_p.reshape(b, 1, 2, N_KV_HEADS, hd, w_len), -1, -2)
    conv_prompt = u_p[:, None, s - (CONV_K - 1):]
    ffn_prompt = zt_p[:, None, zt_p.shape[1] - (FFN_K - 1):]

    xs = x_sample.reshape(1, db, d)
    u_s, q_s, kv_s, g_s, br_s = _proj_step(xs, mods_s, g_norm1[layer], wu, wq, wkv, wg, wbr, ts=db)
    kv_s = kv_s.reshape(db, 6, N_KV_HEADS, hd)
    cact_s = _conv_step(state_conv[:, layer:layer + 1], u_s[0], conv_w[layer], conv_b[layer],
                        conv_ln_g[layer], conv_ln_b[layer])
    cache_t = jnp.swapaxes(cache_kv, -1, -2)
    win_t = jnp.swapaxes(state_win_kv, -1, -2)
    pg = LANE // (PAGE_SIZE // CMP_BLK)
    assert page_table.shape[1] % pg == 0
    rp = pg * PAGE_SIZE
    cmp_past = _compress_pages_t(cache_t, page_table, layer, jnp.tile(pe_t, (1, 1, rp // CMP_BLK)), w1t, w2t,
                                 _pool_matrix(rp), pg=pg)
    ncp = cmp_past.shape[4]
    nnew = 128
    bias_cp = _bias_table(rel_bias, 1, 8, ncp, off=past_len - (CMP_BLK - 1), rs=0, cs=-CMP_BLK)[:, 0, 0]
    bias_cn = _bias_table(rel_bias, 1, 8, nnew, off=past_len - (CMP_BLK - 1) - ncp * CMP_BLK, rs=0, cs=-CMP_BLK)[:, 0, 0]
    gb = _pick(db, (8,))
    oc_s, idx_t = _nsa_step_cmp(q_s.reshape(db, N_HEADS, hd), cmp_past, kv_s[:, :2].reshape(db, 2 * N_KV_HEADS, hd),
                                cmp_pe[layer], cmp_w1[layer], cmp_w2[layer], bias_cp, bias_cn, gb=gb,
                                past_len=past_len)
    n_top = idx_t.shape[1]
    idx = idx_t[:, :, :gb * N_KV_HEADS].reshape(db // gb, n_top, gb, N_KV_HEADS).transpose(0, 2, 3, 1).reshape(db, N_KV_HEADS, n_top)
    ns_tot = (past_len + SEL_BLK) // SEL_BLK
    ns_rows = -(-ns_tot // 8) * 8
    bias_sel = _bias_table(rel_bias, 1, ns_rows, SEL_BLK, off=past_len, rs=-SEL_BLK, cs=-1)
    wcols = -(-(wb + 1) // 128) * 128
    bias_win = _bias_table(rel_bias, 1, 8, wcols, off=wb, rs=0, cs=-1, dmax=WINDOW)[:, 0, 0]
    bias_win = bias_win.reshape(N_KV_HEADS, GROUP, wcols)
    kv_new = kv_s.transpose(0, 2, 1, 3)
    kv_new_t = jnp.pad(kv_s.transpose(0, 2, 3, 1), ((0, 0), (0, 0), (0, 0), (0, 2)))
    o_s = _nsa_step_sel_t(idx, page_table, cache_t, layer, q_s.reshape(db, N_KV_HEADS, GROUP, hd), win_t,
                          kv_new, kv_new_t, oc_s.reshape(db, N_KV_HEADS, GROUP, hd),
                          g_s.reshape(db, 1, -1), bias_sel, bias_win, past_len=past_len)
    o_s = o_s.reshape(1, db, nq).astype(BF16)
    x1_s, h2_s = _merge(cact_s.reshape(1, db, d_conv), o_s, br_s, xs, mods_s, g_norm2[layer], wco, wno, wo,
                        ts=db, nsa_t=False)
    hist_f = state_ffn[:, layer]
    y_s, z_s = _ffn(h2_s, x1_s, mods_s, g_final, wup, ffn_conv_w[layer], ffn_conv_b[layer], wdn,
                           (hist_f[:, 0], hist_f[:, 1]), ts=db, fc=512)
    y_sample = y_s.reshape(db, 1, d)
    kv_sample = kv_s[:, :4].reshape(db, 1, 4, N_KV_HEADS, 1, hd)
    win_sample = jnp.concatenate([state_win_kv[:, layer, :, :, 1:], kv_s[:, 4:, :, None, :]], axis=3)[:, None]
    conv_sample = jnp.concatenate([state_conv[:, layer, 1:], u_s[0][:, None, :]], axis=1)[:, None]
    z_s = z_s[0]
    ffn_sample = jnp.concatenate([hist_f[:, 1:], z_s[:, None, :]], axis=1)[:, None]
    return (y_prompt, y_sample, kv_prompt, kv_sample, win_prompt, win_sample, conv_prompt, conv_sample,
            ffn_prompt, ffn_sample)
```
